```python
import math
import jax, jax.numpy as jnp
from jax import lax
import numpy as np

D_MODEL = 2048
BATCH = 4
SEQ = 2048
DEPTH = 1

CHUNK = 64
Q_BLOCK = 128
MIX_WIDTH = D_MODEL
POOL_WIDTH = MIX_WIDTH // 2
POOL_WINDOWS = (2, 4, 8, 16)
N_POOL_GROUPS = len(POOL_WINDOWS)
POOL_GROUP = POOL_WIDTH // N_POOL_GROUPS
ATTN_WIDTH = MIX_WIDTH - POOL_WIDTH
DIFF_HEAD_DIM = 64
N_DIFF_HEADS = ATTN_WIDTH // (2 * DIFF_HEAD_DIM)
QK_WIDTH = N_DIFF_HEADS * 2 * DIFF_HEAD_DIM
IN_WIDTH = POOL_WIDTH + 2 * QK_WIDTH + ATTN_WIDTH
ROPE_THETA = 10000.0
FFN_HIDDEN = int(math.ceil(8 * D_MODEL / 3 / 256) * 256)
NORM_EPS = 1e-6

kernel_name = "hybrid_pool_diffattn_block"


def rms_norm(x, g):
    xf = x.astype(jnp.float32)
    y = xf * lax.rsqrt(jnp.mean(xf * xf, axis=-1, keepdims=True) + NORM_EPS)
    return (y * g.astype(jnp.float32)).astype(x.dtype)


def rotary_tables(seq, dim):
    inv_freq = ROPE_THETA ** (-jnp.arange(0, dim, 2, dtype=jnp.float32) / dim)
    ang = jnp.arange(seq, dtype=jnp.float32)[:, None] * inv_freq[None, :]
    return jnp.cos(ang), jnp.sin(ang)


def apply_rotary(t, cos, sin):
    half = t.shape[-1] // 2
    t1, t2 = t[..., :half], t[..., half:]
    c = cos[None, :, None, None, :].astype(t.dtype)
    s = sin[None, :, None, None, :].astype(t.dtype)
    return jnp.concatenate([t1 * c - t2 * s, t1 * s + t2 * c], axis=-1)


def multiscale_pool(u, w_pool, pool_scale):
    B, S, _ = u.shape
    uf = u.astype(jnp.float32)
    cs = jnp.concatenate([jnp.zeros((B, 1, POOL_WIDTH), jnp.float32),
                          jnp.cumsum(uf, axis=1)], axis=1)
    pos = jnp.arange(S, dtype=jnp.int32)
    outs = []
    for gi, w in enumerate(POOL_WINDOWS):
        sl = slice(gi * POOL_GROUP, (gi + 1) * POOL_GROUP)
        csg = cs[:, :, sl]
        hi = csg[:, 1:]
        lo = jnp.concatenate([jnp.zeros((B, w - 1, POOL_GROUP), jnp.float32),
                              csg[:, :S - w + 1]], axis=1)
        cnt = jnp.minimum(pos + 1, w).astype(jnp.float32)[None, :, None]
        outs.append((hi - lo) / cnt - uf[:, :, sl])
    pooled = jnp.stack(outs, axis=2).astype(u.dtype)
    mapped = jnp.einsum('bsgc,gcd->bsgd', pooled, w_pool)
    return mapped.reshape(B, S, POOL_WIDTH) * pool_scale


def diff_attention(q, k, v, lam, lam_init, subln_gain):
    B, S = q.shape[0], q.shape[1]
    scale = 1.0 / math.sqrt(DIFF_HEAD_DIM)
    outs = []
    for i in range(S // Q_BLOCK):
        q0, kend = i * Q_BLOCK, (i + 1) * Q_BLOCK
        qb = q[:, q0:kend]
        kb = k[:, :kend]
        vb = v[:, :kend]
        s = jnp.einsum('bqhcd,bkhcd->bhcqk', qb, kb).astype(jnp.float32) * scale
        q_chunk = (q0 + jnp.arange(Q_BLOCK)) // CHUNK
        k_chunk = jnp.arange(kend) // CHUNK
        mask = q_chunk[:, None] >= k_chunk[None, :]
        s = jnp.where(mask[None, None, None], s, -jnp.inf)
        p = jax.nn.softmax(s, axis=-1)
        a = p[:, :, 0] - lam * p[:, :, 1]
        o = jnp.einsum('bhqk,bkhe->bqhe', a.astype(vb.dtype), vb)
        outs.append(o)
    o = jnp.concatenate(outs, axis=1)
    o = rms_norm(o, subln_gain) * (1.0 - lam_init)
    return o.reshape(B, S, ATTN_WIDTH)


def setup_inputs(seed: int = 0) -> dict:
    key = jax.random.key(seed)
    ks = jax.random.split(key, 16)
    f32 = jnp.float32
    nrm = lambda k, shape, s: (jax.random.normal(k, shape, f32) * s)
    L = DEPTH
    return {
        "x": jax.random.normal(ks[0], (BATCH, SEQ, D_MODEL), f32),
        "norm_mix": 1.0 + nrm(ks[1], (L, D_MODEL), 0.02),
        "w_in": nrm(ks[2], (L, D_MODEL, IN_WIDTH), D_MODEL ** -0.5),
        "w_pool": nrm(ks[3], (L, N_POOL_GROUPS, POOL_GROUP, POOL_GROUP), POOL_GROUP ** -0.5),
        "pool_scale": 1.0 + nrm(ks[4], (L, POOL_WIDTH), 0.1),
        "lambda_q1": nrm(ks[5], (L, DIFF_HEAD_DIM), 0.1),
        "lambda_k1": nrm(ks[6], (L, DIFF_HEAD_DIM), 0.1),
        "lambda_q2": nrm(ks[7], (L, DIFF_HEAD_DIM), 0.1),
        "lambda_k2": nrm(ks[8], (L, DIFF_HEAD_DIM), 0.1),
        "subln_gain": 1.0 + nrm(ks[9], (L, 2 * DIFF_HEAD_DIM), 0.02),
        "w_out": nrm(ks[10], (L, MIX_WIDTH, D_MODEL), MIX_WIDTH ** -0.5),
        "norm_ffn": 1.0 + nrm(ks[11], (L, D_MODEL), 0.02),
        "w_gate_up": nrm(ks[12], (L, D_MODEL, 2 * FFN_HIDDEN), D_MODEL ** -0.5),
        "w_down": nrm(ks[13], (L, FFN_HIDDEN, D_MODEL), FFN_HIDDEN ** -0.5),
        "norm_final": 1.0 + nrm(ks[14], (D_MODEL,), 0.02),
    }


def reference(x, norm_mix, w_in, w_pool, pool_scale, lambda_q1, lambda_k1, lambda_q2,
              lambda_k2, subln_gain, w_out, norm_ffn, w_gate_up, w_down, norm_final):
    B, S, _ = x.shape
    cos, sin = rotary_tables(S, DIFF_HEAD_DIM)
    for l in range(DEPTH):
        h = rms_norm(x, norm_mix[l])
        proj = h @ w_in[l]
        u = proj[..., :POOL_WIDTH]
        q = proj[..., POOL_WIDTH:POOL_WIDTH + QK_WIDTH]
        k = proj[..., POOL_WIDTH + QK_WIDTH:POOL_WIDTH + 2 * QK_WIDTH]
        v = proj[..., POOL_WIDTH + 2 * QK_WIDTH:]
        q = apply_rotary(q.reshape(B, S, N_DIFF_HEADS, 2, DIFF_HEAD_DIM), cos, sin)
        k = apply_rotary(k.reshape(B, S, N_DIFF_HEADS, 2, DIFF_HEAD_DIM), cos, sin)
        v = v.reshape(B, S, N_DIFF_HEADS, 2 * DIFF_HEAD_DIM)
        lam_init = 0.8 - 0.6 * math.exp(-0.3 * l)
        lam = (jnp.exp(jnp.sum(lambda_q1[l].astype(jnp.float32) * lambda_k1[l].astype(jnp.float32)))
               - jnp.exp(jnp.sum(lambda_q2[l].astype(jnp.float32) * lambda_k2[l].astype(jnp.float32)))
               + lam_init)
        pool_out = multiscale_pool(u, w_pool[l], pool_scale[l])
        attn_out = diff_attention(q, k, v, lam, lam_init, subln_gain[l])
        mixed = jnp.concatenate([pool_out, attn_out.astype(pool_out.dtype)], axis=-1)
        x = x + mixed @ w_out[l]
        h = rms_norm(x, norm_ffn[l])
        gu = h @ w_gate_up[l]
        gate, up = gu[..., :FFN_HIDDEN], gu[..., FFN_HIDDEN:]
        x = x + (jax.nn.silu(gate) * up) @ w_down[l]
    return rms_norm(x, norm_final)
```

```python
import functools
import math

import jax
import jax.numpy as jnp
from jax import lax
from jax.experimental import pallas as pl
from jax.experimental.pallas import tpu as pltpu

CHUNK = 64
POOL_WINDOWS = (2, 4, 8, 16)
HEAD_DIM = 64
HEAD_WIDTH = 2 * HEAD_DIM
ROPE_THETA = 10000.0
NORM_EPS = 1e-6
LAMBDA_INIT = 0.8 - 0.6 * math.exp(-0.3 * 0)

V7X_VMEM_LIMIT_BYTES = 56 * 1024 * 1024

_NEG_BIG = -1e30


def _params(semantics):
    return pltpu.CompilerParams(dimension_semantics=semantics,
                                vmem_limit_bytes=V7X_VMEM_LIMIT_BYTES)


def _rms_scale(x):
    return x * lax.rsqrt(jnp.mean(x * x, axis=-1, keepdims=True) + NORM_EPS)


def _prenorm_kernel(x_ref, g_ref, o_ref):
    o_ref[...] = (_rms_scale(x_ref[...]) * g_ref[...]).astype(o_ref.dtype)


def _prenorm(x2d, g, tm=512):
    m, d = x2d.shape
    return pl.pallas_call(
        _prenorm_kernel,
        grid=(m // tm,),
        in_specs=[pl.BlockSpec((tm, d), lambda i: (i, 0)),
                  pl.BlockSpec((1, d), lambda i: (0, 0))],
        out_specs=pl.BlockSpec((tm, d), lambda i: (i, 0)),
        out_shape=jax.ShapeDtypeStruct((m, d), jnp.bfloat16),
        compiler_params=_params(("arbitrary",)),
        name="prenorm",
    )(x2d, g.reshape(1, d))


def _in_proj_kernel(a_ref, w_ref, o_ref, wbf_ref):
    @pl.when(pl.program_id(1) == 0)
    def _():
        wbf_ref[...] = w_ref[...].astype(jnp.bfloat16)

    o_ref[...] = jnp.dot(a_ref[...], wbf_ref[...],
                         preferred_element_type=jnp.float32).astype(o_ref.dtype)


def _in_proj(a, w, tm=1024, tn=1024):
    m, k = a.shape
    n = w.shape[1]
    return pl.pallas_call(
        _in_proj_kernel,
        grid=(n // tn, m // tm),
        in_specs=[pl.BlockSpec((tm, k), lambda j, i: (i, 0)),
                  pl.BlockSpec((k, tn), lambda j, i: (0, j))],
        out_specs=pl.BlockSpec((tm, tn), lambda j, i: (i, j)),
        out_shape=jax.ShapeDtypeStruct((m, n), jnp.float32),
        scratch_shapes=[pltpu.VMEM((k, tn), jnp.bfloat16)],
        compiler_params=_params(("arbitrary", "arbitrary")),
        name="in_proj",
    )(a, w)


def _pool_kernel(u_ref, w_ref, s_ref, o_ref):
    g = pl.program_id(1)
    x = u_ref[0]
    seq = x.shape[0]
    row = lax.broadcasted_iota(jnp.int32, x.shape, 0)
    s = x
    for step in range(len(POOL_WINDOWS)):
        shift = 1 << step
        shifted = jnp.where(row >= shift, pltpu.roll(s, shift, axis=0), 0.0)
        s = jnp.where(g >= step, s + shifted, s)
    window = jnp.left_shift(2, g)
    cnt = jnp.minimum(row + 1, window).astype(jnp.float32)
    pooled = s / cnt - x
    mapped = jnp.dot(pooled.astype(jnp.bfloat16), w_ref[0].astype(jnp.bfloat16),
                     preferred_element_type=jnp.float32)
    o_ref[0] = (mapped * s_ref[...]).astype(o_ref.dtype)
    del seq


def _pool(proj3, w_pool, pool_scale):
    b, s, _ = proj3.shape
    ng, cg, _ = w_pool.shape
    return pl.pallas_call(
        _pool_kernel,
        grid=(b, ng),
        in_specs=[pl.BlockSpec((1, s, cg), lambda bi, g: (bi, 0, g)),
                  pl.BlockSpec((1, cg, cg), lambda bi, g: (g, 0, 0)),
                  pl.BlockSpec((1, cg), lambda bi, g: (0, g))],
        out_specs=pl.BlockSpec((1, s, cg), lambda bi, g: (bi, 0, g)),
        out_shape=jax.ShapeDtypeStruct((b, s, ng * cg), jnp.bfloat16),
        compiler_params=_params(("arbitrary", "arbitrary")),
        name="pool",
    )(proj3, w_pool, pool_scale.reshape(1, ng * cg))


def _rope(t, cos_t, sin_t):
    width = t.shape[-1]
    half = HEAD_DIM // 2
    lane = lax.broadcasted_iota(jnp.int32, t.shape, 1)
    first_half = (lane % HEAD_DIM) < half
    partner = jnp.where(first_half,
                        pltpu.roll(t, width - half, axis=1),
                        pltpu.roll(t, half, axis=1))
    return t * cos_t + partner * sin_t


def _attn_kernel(q_ref, k_ref, v_ref, cos_ref, sin_ref, lq1_ref, lk1_ref, lq2_ref,
                 lk2_ref, gain_ref, o_ref, qm_ref, kb_ref, vb_ref, *, tq):
    seq = q_ref.shape[1]
    n_tiles = seq // tq
    scale = 1.0 / math.sqrt(HEAD_DIM)

    cos_t = cos_ref[...]
    sin_t = sin_ref[...]
    lane = lax.broadcasted_iota(jnp.int32, (seq, HEAD_WIDTH), 1)
    q = _rope(q_ref[0], cos_t, sin_t) * scale
    qm_ref[0] = jnp.where(lane < HEAD_DIM, q, 0.0).astype(jnp.bfloat16)
    qm_ref[1] = jnp.where(lane >= HEAD_DIM, q, 0.0).astype(jnp.bfloat16)
    kb_ref[...] = _rope(k_ref[0], cos_t, sin_t).astype(jnp.bfloat16)
    vb_ref[...] = v_ref[0].astype(jnp.bfloat16)

    lam = (jnp.exp(jnp.sum(lq1_ref[...] * lk1_ref[...], axis=-1, keepdims=True))
           - jnp.exp(jnp.sum(lq2_ref[...] * lk2_ref[...], axis=-1, keepdims=True))
           + LAMBDA_INIT)

    r = lax.broadcasted_iota(jnp.int32, (2 * tq, tq), 0) % tq
    c = lax.broadcasted_iota(jnp.int32, (2 * tq, tq), 1)
    diag_mask = (r // CHUNK) >= (c // CHUNK)

    def block(j, carry, q2, mask):
        m, l, acc = carry
        kj = kb_ref[pl.ds(pl.multiple_of(j * tq, tq), tq), :]
        vj = vb_ref[pl.ds(pl.multiple_of(j * tq, tq), tq), :]
        s = lax.dot_general(q2, kj, (((1,), (1,)), ((), ())),
                            preferred_element_type=jnp.float32)
        if mask is not None:
            s = jnp.where(mask, s, _NEG_BIG)
        m_new = jnp.maximum(m, jnp.max(s, axis=-1, keepdims=True))
        alpha = jnp.exp(m - m_new)
        p = jnp.exp(s - m_new)
        l_new = alpha * l + jnp.sum(p, axis=-1, keepdims=True)
        acc_new = alpha * acc + jnp.dot(p.astype(jnp.bfloat16), vj,
                                        preferred_element_type=jnp.float32)
        return m_new, l_new, acc_new

    def q_tile(i, _):
        row0 = pl.multiple_of(i * tq, tq)
        q2 = jnp.concatenate([qm_ref[0, pl.ds(row0, tq), :],
                              qm_ref[1, pl.ds(row0, tq), :]], axis=0)
        init = (jnp.full((2 * tq, 1), _NEG_BIG, jnp.float32),
                jnp.zeros((2 * tq, 1), jnp.float32),
                jnp.zeros((2 * tq, HEAD_WIDTH), jnp.float32))
        carry = lax.fori_loop(0, i, lambda j, cr: block(j, cr, q2, None), init)
        m, l, acc = block(i, carry, q2, diag_mask)
        o = acc / l
        o = o[:tq] - lam * o[tq:]
        o = _rms_scale(o) * gain_ref[...] * (1.0 - LAMBDA_INIT)
        o_ref[0, pl.ds(row0, tq), :] = o.astype(o_ref.dtype)
        return 0

    lax.fori_loop(0, n_tiles, q_tile, 0)


def _attention(proj3, cos_t, sin_t, lq1, lk1, lq2, lk2, gain, n_heads, col0, tq=256):
    b, s, _ = proj3.shape
    hw = HEAD_WIDTH
    qk_blocks = n_heads
    q0 = col0 // hw
    vec = lambda a: a.reshape(1, -1)
    small = lambda n: pl.BlockSpec((1, n), lambda bi, h: (0, 0))
    return pl.pallas_call(
        functools.partial(_attn_kernel, tq=tq),
        grid=(b, n_heads),
        in_specs=[pl.BlockSpec((1, s, hw), lambda bi, h: (bi, 0, q0 + h)),
                  pl.BlockSpec((1, s, hw), lambda bi, h: (bi, 0, q0 + qk_blocks + h)),
                  pl.BlockSpec((1, s, hw), lambda bi, h: (bi, 0, q0 + 2 * qk_blocks + h)),
                  pl.BlockSpec((s, hw), lambda bi, h: (0, 0)),
                  pl.BlockSpec((s, hw), lambda bi, h: (0, 0)),
                  small(HEAD_DIM), small(HEAD_DIM), small(HEAD_DIM), small(HEAD_DIM),
                  small(hw)],
        out_specs=pl.BlockSpec((1, s, hw), lambda bi, h: (bi, 0, h)),
        out_shape=jax.ShapeDtypeStruct((b, s, n_heads * hw), jnp.bfloat16),
        scratch_shapes=[pltpu.VMEM((2, s, hw), jnp.bfloat16),
                        pltpu.VMEM((s, hw), jnp.bfloat16),
                        pltpu.VMEM((s, hw), jnp.bfloat16)],
        compiler_params=_params(("arbitrary", "arbitrary")),
        name="diff_attn",
    )(proj3, proj3, proj3, cos_t, sin_t, vec(lq1), vec(lk1), vec(lq2), vec(lk2), vec(gain))


def _out_proj_kernel(p_ref, a_ref, x_ref, w_ref, g_ref, x1_ref, h_ref, wbf_ref):
    @pl.when(pl.program_id(0) == 0)
    def _():
        wbf_ref[...] = w_ref[...].astype(jnp.bfloat16)

    kp = p_ref.shape[1]
    y = jnp.dot(p_ref[...], wbf_ref[:kp, :], preferred_element_type=jnp.float32)
    y += jnp.dot(a_ref[...], wbf_ref[kp:, :], preferred_element_type=jnp.float32)
    x1 = x_ref[...] + y
    x1_ref[...] = x1
    h_ref[...] = (_rms_scale(x1) * g_ref[...]).astype(h_ref.dtype)


def _out_proj(pool_out, attn_out, x2d, w_out, g, tm=512):
    m, d = x2d.shape
    kp, ka = pool_out.shape[1], attn_out.shape[1]
    return pl.pallas_call(
        _out_proj_kernel,
        grid=(m // tm,),
        in_specs=[pl.BlockSpec((tm, kp), lambda i: (i, 0)),
                  pl.BlockSpec((tm, ka), lambda i: (i, 0)),
                  pl.BlockSpec((tm, d), lambda i: (i, 0)),
                  pl.BlockSpec((kp + ka, d), lambda i: (0, 0),
                               pipeline_mode=pl.Buffered(1)),
                  pl.BlockSpec((1, d), lambda i: (0, 0))],
        out_specs=[pl.BlockSpec((tm, d), lambda i: (i, 0)),
                   pl.BlockSpec((tm, d), lambda i: (i, 0))],
        out_shape=[jax.ShapeDtypeStruct((m, d), jnp.float32),
                   jax.ShapeDtypeStruct((m, d), jnp.bfloat16)],
        scratch_shapes=[pltpu.VMEM((kp + ka, d), jnp.bfloat16)],
        compiler_params=_params(("arbitrary",)),
        name="out_proj",
    )(pool_out, attn_out, x2d, w_out, g.reshape(1, d))


def _gate_up_kernel(a_ref, wg_ref, wu_ref, o_ref, wgb_ref, wub_ref):
    @pl.when(pl.program_id(1) == 0)
    def _():
        wgb_ref[...] = wg_ref[...].astype(jnp.bfloat16)
        wub_ref[...] = wu_ref[...].astype(jnp.bfloat16)

    a = a_ref[...]
    gate = jnp.dot(a, wgb_ref[...], preferred_element_type=jnp.float32)
    up = jnp.dot(a, wub_ref[...], preferred_element_type=jnp.float32)
    o_ref[...] = (gate * jax.nn.sigmoid(gate) * up).astype(o_ref.dtype)


def _gate_up(a, w_gate_up, hidden, tm=1024, th=512):
    m, k = a.shape
    nb = hidden // th
    return pl.pallas_call(
        _gate_up_kernel,
        grid=(nb, m // tm),
        in_specs=[pl.BlockSpec((tm, k), lambda j, i: (i, 0)),
                  pl.BlockSpec((k, th), lambda j, i: (0, j)),
                  pl.BlockSpec((k, th), lambda j, i: (0, j + nb))],
        out_specs=pl.BlockSpec((tm, th), lambda j, i: (i, j)),
        out_shape=jax.ShapeDtypeStruct((m, hidden), jnp.bfloat16),
        scratch_shapes=[pltpu.VMEM((k, th), jnp.bfloat16),
                        pltpu.VMEM((k, th), jnp.bfloat16)],
        compiler_params=_params(("arbitrary", "arbitrary")),
        name="gate_up",
    )(a, w_gate_up, w_gate_up)


def _down_kernel(a_ref, w_ref, x1_ref, g_ref, o_ref, acc_ref):
    kk = pl.program_id(1)

    @pl.when(kk == 0)
    def _():
        acc_ref[...] = x1_ref[...]

    acc_ref[...] += jnp.dot(a_ref[...], w_ref[...].astype(jnp.bfloat16),
                            preferred_element_type=jnp.float32)

    @pl.when(kk == pl.num_programs(1) - 1)
    def _():
        o_ref[...] = _rms_scale(acc_ref[...]) * g_ref[...]


def _down(a, w_down, x1, g, tm=1024, tk=512):
    m, hidden = a.shape
    d = w_down.shape[1]
    return pl.pallas_call(
        _down_kernel,
        grid=(m // tm, hidden // tk),
        in_specs=[pl.BlockSpec((tm, tk), lambda i, kk: (i, kk)),
                  pl.BlockSpec((tk, d), lambda i, kk: (kk, 0)),
                  pl.BlockSpec((tm, d), lambda i, kk: (i, 0)),
                  pl.BlockSpec((1, d), lambda i, kk: (0, 0))],
        out_specs=pl.BlockSpec((tm, d), lambda i, kk: (i, 0)),
        out_shape=jax.ShapeDtypeStruct((m, d), jnp.float32),
        scratch_shapes=[pltpu.VMEM((tm, d), jnp.float32)],
        compiler_params=_params(("arbitrary", "arbitrary")),
        name="down",
    )(a, w_down, x1, g.reshape(1, d))


def _rotary_tables(seq):
    inv_freq = ROPE_THETA ** (-jnp.arange(0, HEAD_DIM, 2, dtype=jnp.float32) / HEAD_DIM)
    ang = jnp.arange(seq, dtype=jnp.float32)[:, None] * inv_freq[None, :]
    cos, sin = jnp.cos(ang), jnp.sin(ang)
    reps = HEAD_WIDTH // HEAD_DIM
    cos_t = jnp.tile(jnp.concatenate([cos, cos], axis=-1), (1, reps))
    sin_t = jnp.tile(jnp.concatenate([-sin, sin], axis=-1), (1, reps))
    return cos_t, sin_t


def kernel(x, norm_mix, w_in, w_pool, pool_scale, lambda_q1, lambda_k1, lambda_q2,
           lambda_k2, subln_gain, w_out, norm_ffn, w_gate_up, w_down, norm_final):
    b, s, d = x.shape
    depth = w_in.shape[0]
    assert depth == 1, "single-layer block"
    pool_width = w_pool.shape[1] * w_pool.shape[2]
    hidden = w_down.shape[1]
    attn_width = w_out.shape[1] - pool_width
    n_heads = attn_width // HEAD_WIDTH

    cos_t, sin_t = _rotary_tables(s)
    x2d = x.reshape(b * s, d)
    l = 0
    h1 = _prenorm(x2d, norm_mix[l])
    proj = _in_proj(h1, w_in[l]).reshape(b, s, -1)
    pool_out = _pool(proj, w_pool[l], pool_scale[l])
    attn_out = _attention(proj, cos_t, sin_t, lambda_q1[l], lambda_k1[l], lambda_q2[l],
                          lambda_k2[l], subln_gain[l], n_heads, pool_width)
    x1, h2 = _out_proj(pool_out.reshape(b * s, -1), attn_out.reshape(b * s, -1), x2d,
                       w_out[l], norm_ffn[l])
    a = _gate_up(h2, w_gate_up[l], hidden)
    out = _down(a, w_down[l], x1, norm_final)
    return out.reshape(b, s, d)
```

```python
import functools
import math

import jax
import jax.numpy as jnp
from jax import lax
from jax.experimental import pallas as pl
from jax.experimental.pallas import tpu as pltpu

CHUNK = 64
POOL_WINDOWS = (2, 4, 8, 16)
HEAD_DIM = 64
HEAD_WIDTH = 2 * HEAD_DIM
ROPE_THETA = 10000.0
NORM_EPS = 1e-6
LAMBDA_INIT = 0.8 - 0.6 * math.exp(-0.3 * 0)

V7X_VMEM_LIMIT_BYTES = 56 * 1024 * 1024

_NEG_BIG = -1e30


def _params(semantics):
    return pltpu.CompilerParams(dimension_semantics=semantics,
                                vmem_limit_bytes=V7X_VMEM_LIMIT_BYTES)


def _rms_scale(x):
    return x * lax.rsqrt(jnp.mean(x * x, axis=-1, keepdims=True) + NORM_EPS)


def _prenorm_kernel(x_ref, g_ref, o_ref):
    o_ref[...] = (_rms_scale(x_ref[...]) * g_ref[...]).astype(o_ref.dtype)


def _prenorm(x2d, g, tm=512):
    m, d = x2d.shape
    return pl.pallas_call(
        _prenorm_kernel,
        grid=(m // tm,),
        in_specs=[pl.BlockSpec((tm, d), lambda i: (i, 0)),
                  pl.BlockSpec((1, d), lambda i: (0, 0))],
        out_specs=pl.BlockSpec((tm, d), lambda i: (i, 0)),
        out_shape=jax.ShapeDtypeStruct((m, d), jnp.bfloat16),
        compiler_params=_params(("arbitrary",)),
        name="prenorm",
    )(x2d, g.reshape(1, d))


def _rope(t, cos_t, sin_t):
    width = t.shape[-1]
    half = HEAD_DIM // 2
    lane = lax.broadcasted_iota(jnp.int32, t.shape, 1)
    first_half = (lane % HEAD_DIM) < half
    partner = jnp.where(first_half,
                        pltpu.roll(t, width - half, axis=1),
                        pltpu.roll(t, half, axis=1))
    return t * cos_t + partner * sin_t


def _in_proj_kernel(a_ref, w_ref, cos_ref, sin_ref, o_ref, wbf_ref, *, n_q_tiles,
                    n_rope_tiles):
    j = pl.program_id(0)

    @pl.when(pl.program_id(1) == 0)
    def _():
        wbf_ref[...] = w_ref[...].astype(jnp.bfloat16)

    y = jnp.dot(a_ref[...], wbf_ref[...], preferred_element_type=jnp.float32)

    if n_rope_tiles == 0:
        o_ref[...] = y.astype(o_ref.dtype)
        return

    @pl.when(j < n_rope_tiles)
    def _():
        scale = jnp.where(j < n_q_tiles, 1.0 / math.sqrt(HEAD_DIM), 1.0)
        cos_t = cos_ref[...] * scale
        sin_t = sin_ref[...] * scale
        for h in range(y.shape[1] // HEAD_WIDTH):
            cols = slice(h * HEAD_WIDTH, (h + 1) * HEAD_WIDTH)
            o_ref[:, cols] = _rope(y[:, cols], cos_t, sin_t).astype(o_ref.dtype)

    @pl.when(j >= n_rope_tiles)
    def _():
        o_ref[...] = y.astype(o_ref.dtype)


def _in_proj(a, w, cos_t, sin_t, *, col0, ncols, out_dtype, q_cols=0, k_cols=0,
             tm=1024, tn=1024):
    m, k = a.shape
    seq = cos_t.shape[0]
    assert col0 % tn == 0 and q_cols % tn == 0 and k_cols % tn == 0 and seq % tm == 0
    jb = col0 // tn
    sb = seq // tm
    n_q_tiles = q_cols // tn
    n_rope_tiles = (q_cols + k_cols) // tn
    return pl.pallas_call(
        functools.partial(_in_proj_kernel, n_q_tiles=n_q_tiles, n_rope_tiles=n_rope_tiles),
        grid=(ncols // tn, m // tm),
        in_specs=[pl.BlockSpec((tm, k), lambda j, i: (i, 0)),
                  pl.BlockSpec((k, tn), lambda j, i: (0, j + jb)),
                  pl.BlockSpec((tm, HEAD_WIDTH), lambda j, i: (i % sb, 0)),
                  pl.BlockSpec((tm, HEAD_WIDTH), lambda j, i: (i % sb, 0))],
        out_specs=pl.BlockSpec((tm, tn), lambda j, i: (i, j)),
        out_shape=jax.ShapeDtypeStruct((m, ncols), out_dtype),
        scratch_shapes=[pltpu.VMEM((k, tn), jnp.bfloat16)],
        compiler_params=_params(("arbitrary", "arbitrary")),
        name="in_proj_qkv" if n_rope_tiles else "in_proj_u",
    )(a, w, cos_t, sin_t)


def _pool_kernel(u_ref, w_ref, s_ref, o_ref):
    g = pl.program_id(1)
    x = u_ref[0]
    seq = x.shape[0]
    row = lax.broadcasted_iota(jnp.int32, x.shape, 0)
    s = x
    for step in range(len(POOL_WINDOWS)):
        shift = 1 << step
        shifted = jnp.where(row >= shift, pltpu.roll(s, shift, axis=0), 0.0)
        s = jnp.where(g >= step, s + shifted, s)
    window = jnp.left_shift(2, g)
    cnt = jnp.minimum(row + 1, window).astype(jnp.float32)
    pooled = s / cnt - x
    mapped = jnp.dot(pooled.astype(jnp.bfloat16), w_ref[0].astype(jnp.bfloat16),
                     preferred_element_type=jnp.float32)
    o_ref[0] = (mapped * s_ref[...]).astype(o_ref.dtype)
    del seq


def _pool(proj3, w_pool, pool_scale):
    b, s, _ = proj3.shape
    ng, cg, _ = w_pool.shape
    return pl.pallas_call(
        _pool_kernel,
        grid=(b, ng),
        in_specs=[pl.BlockSpec((1, s, cg), lambda bi, g: (bi, 0, g)),
                  pl.BlockSpec((1, cg, cg), lambda bi, g: (g, 0, 0)),
                  pl.BlockSpec((1, cg), lambda bi, g: (0, g))],
        out_specs=pl.BlockSpec((1, s, cg), lambda bi, g: (bi, 0, g)),
        out_shape=jax.ShapeDtypeStruct((b, s, ng * cg), jnp.bfloat16),
        compiler_params=_params(("arbitrary", "arbitrary")),
        name="pool",
    )(proj3, w_pool, pool_scale.reshape(1, ng * cg))


def _attn_kernel(q_ref, k_ref, v_ref, lq1_ref, lk1_ref, lq2_ref, lk2_ref, gain_ref,
                 o_ref, vt_ref, st_ref, *, tq):
    seq = q_ref.shape[1]
    n_tiles = seq // tq

    vt_ref[...] = v_ref[0].astype(jnp.float32).T.astype(jnp.bfloat16)

    lam = (jnp.exp(jnp.sum(lq1_ref[...] * lk1_ref[...], axis=-1, keepdims=True))
           - jnp.exp(jnp.sum(lq2_ref[...] * lk2_ref[...], axis=-1, keepdims=True))
           + LAMBDA_INIT)

    lane = lax.broadcasted_iota(jnp.int32, (tq, HEAD_WIDTH), 1)
    key = lax.broadcasted_iota(jnp.int32, (tq, 2 * tq), 0)
    qry = lax.broadcasted_iota(jnp.int32, (tq, 2 * tq), 1) % tq
    diag_mask = (qry // CHUNK) >= (key // CHUNK)

    for i in range(n_tiles):
        row0, kend = i * tq, (i + 1) * tq
        q = q_ref[0, row0:kend, :]
        q2 = jnp.concatenate([jnp.where(lane < HEAD_DIM, q, 0),
                              jnp.where(lane >= HEAD_DIM, q, 0)], axis=0)
        st = lax.dot_general(k_ref[0, :kend, :], q2, (((1,), (1,)), ((), ())),
                             preferred_element_type=jnp.float32)
        if i:
            st_ref[:row0, :] = st[:row0]
        sd = jnp.where(diag_mask, st[row0:], _NEG_BIG)
        st_ref[row0:kend, :] = sd
        m = jnp.max(sd, axis=0, keepdims=True)
        if i:
            m = jnp.maximum(m, jnp.max(st[:row0], axis=0, keepdims=True))
        p = jnp.exp(st_ref[:kend, :] - m)
        inv_l = 1.0 / jnp.sum(p, axis=0, keepdims=True)
        ot = jnp.dot(vt_ref[:, :kend], p.astype(jnp.bfloat16),
                     preferred_element_type=jnp.float32) * inv_l
        o = (ot[:, :tq] - lam * ot[:, tq:]).T
        o = _rms_scale(o) * gain_ref[...] * (1.0 - LAMBDA_INIT)
        o_ref[0, row0:kend, :] = o.astype(o_ref.dtype)


def _attention(qkv3, lq1, lk1, lq2, lk2, gain, n_heads, tq=256):
    b, s, _ = qkv3.shape
    hw = HEAD_WIDTH
    vec = lambda a: a.reshape(1, -1)
    small = lambda n: pl.BlockSpec((1, n), lambda bi, h: (0, 0))
    return pl.pallas_call(
        functools.partial(_attn_kernel, tq=tq),
        grid=(b, n_heads),
        in_specs=[pl.BlockSpec((1, s, hw), lambda bi, h: (bi, 0, h)),
                  pl.BlockSpec((1, s, hw), lambda bi, h: (bi, 0, n_heads + h)),
                  pl.BlockSpec((1, s, hw), lambda bi, h: (bi, 0, 2 * n_heads + h)),
                  small(HEAD_DIM), small(HEAD_DIM), small(HEAD_DIM), small(HEAD_DIM),
                  small(hw)],
        out_specs=pl.BlockSpec((1, s, hw), lambda bi, h: (bi, 0, h)),
        out_shape=jax.ShapeDtypeStruct((b, s, n_heads * hw), jnp.bfloat16),
        scratch_shapes=[pltpu.VMEM((hw, s), jnp.bfloat16),
                        pltpu.VMEM((s, 2 * tq), jnp.float32)],
        compiler_params=_params(("arbitrary", "arbitrary")),
        name="diff_attn",
    )(qkv3, qkv3, qkv3, vec(lq1), vec(lk1), vec(lq2), vec(lk2), vec(gain))


def _out_proj_kernel(p_ref, a_ref, x_ref, w_ref, g_ref, x1_ref, h_ref, wbf_ref):
    @pl.when(pl.program_id(0) == 0)
    def _():
        wbf_ref[...] = w_ref[...].astype(jnp.bfloat16)

    kp = p_ref.shape[1]
    y = jnp.dot(p_ref[...], wbf_ref[:kp, :], preferred_element_type=jnp.float32)
    y += jnp.dot(a_ref[...], wbf_ref[kp:, :], preferred_element_type=jnp.float32)
    x1 = x_ref[...] + y
    x1_ref[...] = x1
    h_ref[...] = (_rms_scale(x1) * g_ref[...]).astype(h_ref.dtype)


def _out_proj(pool_out, attn_out, x2d, w_out, g, tm=512):
    m, d = x2d.shape
    kp, ka = pool_out.shape[1], attn_out.shape[1]
    return pl.pallas_call(
        _out_proj_kernel,
        grid=(m // tm,),
        in_specs=[pl.BlockSpec((tm, kp), lambda i: (i, 0)),
                  pl.BlockSpec((tm, ka), lambda i: (i, 0)),
                  pl.BlockSpec((tm, d), lambda i: (i, 0)),
                  pl.BlockSpec((kp + ka, d), lambda i: (0, 0),
                               pipeline_mode=pl.Buffered(1)),
                  pl.BlockSpec((1, d), lambda i: (0, 0))],
        out_specs=[pl.BlockSpec((tm, d), lambda i: (i, 0)),
                   pl.BlockSpec((tm, d), lambda i: (i, 0))],
        out_shape=[jax.ShapeDtypeStruct((m, d), jnp.float32),
                   jax.ShapeDtypeStruct((m, d), jnp.bfloat16)],
        scratch_shapes=[pltpu.VMEM((kp + ka, d), jnp.bfloat16)],
        compiler_params=_params(("arbitrary",)),
        name="out_proj",
    )(pool_out, attn_out, x2d, w_out, g.reshape(1, d))


def _gate_up_kernel(a_ref, wg_ref, wu_ref, o_ref, wgb_ref, wub_ref):
    @pl.when(pl.program_id(1) == 0)
    def _():
        wgb_ref[...] = wg_ref[...].astype(jnp.bfloat16)
        wub_ref[...] = wu_ref[...].astype(jnp.bfloat16)

    a = a_ref[...]
    gate = jnp.dot(a, wgb_ref[...], preferred_element_type=jnp.float32)
    up = jnp.dot(a, wub_ref[...], preferred_element_type=jnp.float32)
    o_ref[...] = (gate * jax.nn.sigmoid(gate) * up).astype(o_ref.dtype)


def _gate_up(a, w_gate_up, hidden, tm=1024, th=512):
    m, k = a.shape
    nb = hidden // th
    return pl.pallas_call(
        _gate_up_kernel,
        grid=(nb, m // tm),
        in_specs=[pl.BlockSpec((tm, k), lambda j, i: (i, 0)),
                  pl.BlockSpec((k, th), lambda j, i: (0, j)),
                  pl.BlockSpec((k, th), lambda j, i: (0, j + nb))],
        out_specs=pl.BlockSpec((tm, th), lambda j, i: (i, j)),
        out_shape=jax.ShapeDtypeStruct((m, hidden), jnp.bfloat16),
        scratch_shapes=[pltpu.VMEM((k, th), jnp.bfloat16),
                        pltpu.VMEM((k, th), jnp.bfloat16)],
        compiler_params=_params(("arbitrary", "arbitrary")),
        name="gate_up",
    )(a, w_gate_up, w_gate_up)


def _down_kernel(a_ref, w_ref, x1_ref, g_ref, o_ref, acc_ref):
    kk = pl.program_id(1)

    @pl.when(kk == 0)
    def _():
        acc_ref[...] = x1_ref[...]

    acc_ref[...] += jnp.dot(a_ref[...], w_ref[...].astype(jnp.bfloat16),
                            preferred_element_type=jnp.float32)

    @pl.when(kk == pl.num_programs(1) - 1)
    def _():
        o_ref[...] = _rms_scale(acc_ref[...]) * g_ref[...]


def _down(a, w_down, x1, g, tm=1024, tk=512):
    m, hidden = a.shape
    d = w_down.shape[1]
    return pl.pallas_call(
        _down_kernel,
        grid=(m // tm, hidden // tk),
        in_specs=[pl.BlockSpec((tm, tk), lambda i, kk: (i, kk)),
                  pl.BlockSpec((tk, d), lambda i, kk: (kk, 0)),
                  pl.BlockSpec((tm, d), lambda i, kk: (i, 0)),
                  pl.BlockSpec((1, d), lambda i, kk: (0, 0))],
        out_specs=pl.BlockSpec((tm, d), lambda i, kk: (i, 0)),
        out_shape=jax.ShapeDtypeStruct((m, d), jnp.float32),
        scratch_shapes=[pltpu.VMEM((tm, d), jnp.float32)],
        compiler_params=_params(("arbitrary", "arbitrary")),
        name="down",
    )(a, w_down, x1, g.reshape(1, d))


def _rotary_tables(seq):
    inv_freq = ROPE_THETA ** (-jnp.arange(0, HEAD_DIM, 2, dtype=jnp.float32) / HEAD_DIM)
    ang = jnp.arange(seq, dtype=jnp.float32)[:, None] * inv_freq[None, :]
    cos, sin = jnp.cos(ang), jnp.sin(ang)
    reps = HEAD_WIDTH // HEAD_DIM
    cos_t = jnp.tile(jnp.concatenate([cos, cos], axis=-1), (1, reps))
    sin_t = jnp.tile(jnp.concatenate([-sin, sin], axis=-1), (1, reps))
    return cos_t, sin_t


def kernel(x, norm_mix, w_in, w_pool, pool_scale, lambda_q1, lambda_k1, lambda_q2,
           lambda_k2, subln_gain, w_out, norm_ffn, w_gate_up, w_down, norm_final):
    b, s, d = x.shape
    depth = w_in.shape[0]
    assert depth == 1, "single-layer block"
    pool_width = w_pool.shape[1] * w_pool.shape[2]
    hidden = w_down.shape[1]
    attn_width = w_out.shape[1] - pool_width
    n_heads = attn_width // HEAD_WIDTH

    cos_t, sin_t = _rotary_tables(s)
    x2d = x.reshape(b * s, d)
    l = 0
    h1 = _prenorm(x2d, norm_mix[l])
    qk_width = n_heads * HEAD_WIDTH
    u = _in_proj(h1, w_in[l], cos_t, sin_t, col0=0, ncols=pool_width,
                 out_dtype=jnp.float32)
    qkv = _in_proj(h1, w_in[l], cos_t, sin_t, col0=pool_width, ncols=2 * qk_width + attn_width,
                   out_dtype=jnp.bfloat16, q_cols=qk_width, k_cols=qk_width)
    pool_out = _pool(u.reshape(b, s, -1), w_pool[l], pool_scale[l])
    attn_out = _attention(qkv.reshape(b, s, -1), lambda_q1[l], lambda_k1[l], lambda_q2[l],
                          lambda_k2[l], subln_gain[l], n_heads)
    x1, h2 = _out_proj(pool_out.reshape(b * s, -1), attn_out.reshape(b * s, -1), x2d,
                       w_out[l], norm_ffn[l])
    a = _gate_up(h2, w_gate_up[l], hidden)
    out = _down(a, w_down[l], x1, norm_final)
    return out.reshape(b, s, d)
```

```python
import functools
import math

import jax
import jax.numpy as jnp
from jax import lax
from jax.experimental import pallas as pl
from jax.experimental.pallas import tpu as pltpu

CHUNK = 64
POOL_WINDOWS = (2, 4, 8, 16)
HEAD_DIM = 64
HEAD_WIDTH = 2 * HEAD_DIM
ROPE_THETA = 10000.0
NORM_EPS = 1e-6
LAMBDA_INIT = 0.8 - 0.6 * math.exp(-0.3 * 0)

V7X_VMEM_LIMIT_BYTES = 56 * 1024 * 1024

_NEG_BIG = -1e30


def _params(semantics):
    return pltpu.CompilerParams(dimension_semantics=semantics,
                                vmem_limit_bytes=V7X_VMEM_LIMIT_BYTES)


def _rms_scale(x):
    return x * lax.rsqrt(jnp.mean(x * x, axis=-1, keepdims=True) + NORM_EPS)


def _prenorm_kernel(x_ref, g_ref, o_ref):
    o_ref[...] = (_rms_scale(x_ref[...]) * g_ref[...]).astype(o_ref.dtype)


def _prenorm(x2d, g, tm=512):
    m, d = x2d.shape
    return pl.pallas_call(
        _prenorm_kernel,
        grid=(m // tm,),
        in_specs=[pl.BlockSpec((tm, d), lambda i: (i, 0)),
                  pl.BlockSpec((1, d), lambda i: (0, 0))],
        out_specs=pl.BlockSpec((tm, d), lambda i: (i, 0)),
        out_shape=jax.ShapeDtypeStruct((m, d), jnp.bfloat16),
        compiler_params=_params(("arbitrary",)),
        name="prenorm",
    )(x2d, g.reshape(1, d))


def _rope(t, cos_t, sin_t):
    width = t.shape[-1]
    half = HEAD_DIM // 2
    lane = lax.broadcasted_iota(jnp.int32, t.shape, 1)
    first_half = (lane % HEAD_DIM) < half
    partner = jnp.where(first_half,
                        pltpu.roll(t, width - half, axis=1),
                        pltpu.roll(t, half, axis=1))
    return t * cos_t + partner * sin_t


def _in_proj_kernel(a_ref, w_ref, o_ref, wbf_ref):
    @pl.when(pl.program_id(1) == 0)
    def _():
        wbf_ref[...] = w_ref[...].astype(jnp.bfloat16)

    o_ref[...] = jnp.dot(a_ref[...], wbf_ref[...],
                         preferred_element_type=jnp.float32).astype(o_ref.dtype)


def _in_proj_rope_kernel(a_ref, w_ref, cos_ref, sin_ref, o_ref, wbf_ref, *, n_q_tiles):
    @pl.when(pl.program_id(1) == 0)
    def _():
        wbf_ref[...] = w_ref[...].astype(jnp.bfloat16)

    y = jnp.dot(a_ref[...], wbf_ref[...], preferred_element_type=jnp.float32)
    scale = jnp.where(pl.program_id(0) < n_q_tiles, 1.0 / math.sqrt(HEAD_DIM), 1.0)
    cos_t = cos_ref[...] * scale
    sin_t = sin_ref[...] * scale
    for h in range(y.shape[1] // HEAD_WIDTH):
        cols = slice(h * HEAD_WIDTH, (h + 1) * HEAD_WIDTH)
        o_ref[:, cols] = _rope(y[:, cols], cos_t, sin_t).astype(o_ref.dtype)


def _in_proj(a, w, *, col0, ncols, out_dtype, name, rope=None, tm=1024, tn=1024):
    m, k = a.shape
    assert col0 % tn == 0
    jb = col0 // tn
    in_specs = [pl.BlockSpec((tm, k), lambda j, i: (i, 0)),
                pl.BlockSpec((k, tn), lambda j, i: (0, j + jb))]
    args = (a, w)
    body = _in_proj_kernel
    if rope is not None:
        cos_t, sin_t, q_cols = rope
        seq = cos_t.shape[0]
        assert q_cols % tn == 0 and seq % tm == 0
        sb = seq // tm
        in_specs += [pl.BlockSpec((tm, HEAD_WIDTH), lambda j, i: (i % sb, 0)),
                     pl.BlockSpec((tm, HEAD_WIDTH), lambda j, i: (i % sb, 0))]
        args += (cos_t, sin_t)
        body = functools.partial(_in_proj_rope_kernel, n_q_tiles=q_cols // tn)
    return pl.pallas_call(
        body,
        grid=(ncols // tn, m // tm),
        in_specs=in_specs,
        out_specs=pl.BlockSpec((tm, tn), lambda j, i: (i, j)),
        out_shape=jax.ShapeDtypeStruct((m, ncols), out_dtype),
        scratch_shapes=[pltpu.VMEM((k, tn), jnp.bfloat16)],
        compiler_params=_params(("arbitrary", "arbitrary")),
        name=name,
    )(*args)


def _pool_kernel(u_ref, w_ref, s_ref, o_ref):
    g = pl.program_id(1)
    x = u_ref[0]
    seq = x.shape[0]
    row = lax.broadcasted_iota(jnp.int32, x.shape, 0)
    s = x
    for step in range(len(POOL_WINDOWS)):
        shift = 1 << step
        shifted = jnp.where(row >= shift, pltpu.roll(s, shift, axis=0), 0.0)
        s = jnp.where(g >= step, s + shifted, s)
    window = jnp.left_shift(2, g)
    cnt = jnp.minimum(row + 1, window).astype(jnp.float32)
    pooled = s / cnt - x
    mapped = jnp.dot(pooled.astype(jnp.bfloat16), w_ref[0].astype(jnp.bfloat16),
                     preferred_element_type=jnp.float32)
    o_ref[0] = (mapped * s_ref[...]).astype(o_ref.dtype)
    del seq


def _pool(proj3, w_pool, pool_scale):
    b, s, _ = proj3.shape
    ng, cg, _ = w_pool.shape
    return pl.pallas_call(
        _pool_kernel,
        grid=(b, ng),
        in_specs=[pl.BlockSpec((1, s, cg), lambda bi, g: (bi, 0, g)),
                  pl.BlockSpec((1, cg, cg), lambda bi, g: (g, 0, 0)),
                  pl.BlockSpec((1, cg), lambda bi, g: (0, g))],
        out_specs=pl.BlockSpec((1, s, cg), lambda bi, g: (bi, 0, g)),
        out_shape=jax.ShapeDtypeStruct((b, s, ng * cg), jnp.bfloat16),
        compiler_params=_params(("arbitrary", "arbitrary")),
        name="pool",
    )(proj3, w_pool, pool_scale.reshape(1, ng * cg))


def _attn_kernel(q_ref, k_ref, v_ref, lq1_ref, lk1_ref, lq2_ref, lk2_ref, gain_ref,
                 o_ref, vt_ref, st_ref, *, tq):
    seq = q_ref.shape[1]
    n_tiles = seq // tq

    vt_ref[...] = v_ref[0].astype(jnp.float32).T.astype(jnp.bfloat16)

    lam = (jnp.exp(jnp.sum(lq1_ref[...] * lk1_ref[...], axis=-1, keepdims=True))
           - jnp.exp(jnp.sum(lq2_ref[...] * lk2_ref[...], axis=-1, keepdims=True))
           + LAMBDA_INIT)

    lane = lax.broadcasted_iota(jnp.int32, (tq, HEAD_WIDTH), 1)
    key = lax.broadcasted_iota(jnp.int32, (tq, 2 * tq), 0)
    qry = lax.broadcasted_iota(jnp.int32, (tq, 2 * tq), 1) % tq
    diag_mask = (qry // CHUNK) >= (key // CHUNK)

    for i in range(n_tiles):
        row0, kend = i * tq, (i + 1) * tq
        q = q_ref[0, row0:kend, :]
        q2 = jnp.concatenate([jnp.where(lane < HEAD_DIM, q, 0),
                              jnp.where(lane >= HEAD_DIM, q, 0)], axis=0)
        st = lax.dot_general(k_ref[0, :kend, :], q2, (((1,), (1,)), ((), ())),
                             preferred_element_type=jnp.float32)
        if i:
            st_ref[:row0, :] = st[:row0]
        sd = jnp.where(diag_mask, st[row0:], _NEG_BIG)
        st_ref[row0:kend, :] = sd
        m = jnp.max(sd, axis=0, keepdims=True)
        if i:
            m = jnp.maximum(m, jnp.max(st[:row0], axis=0, keepdims=True))
        p = jnp.exp(st_ref[:kend, :] - m)
        inv_l = 1.0 / jnp.sum(p, axis=0, keepdims=True)
        ot = jnp.dot(vt_ref[:, :kend], p.astype(jnp.bfloat16),
                     preferred_element_type=jnp.float32) * inv_l
        o = (ot[:, :tq] - lam * ot[:, tq:]).T
        o = _rms_scale(o) * gain_ref[...] * (1.0 - LAMBDA_INIT)
        o_ref[0, row0:kend, :] = o.astype(o_ref.dtype)


def _attention(qk3, v3, lq1, lk1, lq2, lk2, gain, n_heads, tq=256):
    b, s, _ = v3.shape
    hw = HEAD_WIDTH
    vec = lambda a: a.reshape(1, -1)
    small = lambda n: pl.BlockSpec((1, n), lambda bi, h: (0, 0))
    return pl.pallas_call(
        functools.partial(_attn_kernel, tq=tq),
        grid=(b, n_heads),
        in_specs=[pl.BlockSpec((1, s, hw), lambda bi, h: (bi, 0, h)),
                  pl.BlockSpec((1, s, hw), lambda bi, h: (bi, 0, n_heads + h)),
                  pl.BlockSpec((1, s, hw), lambda bi, h: (bi, 0, h)),
                  small(HEAD_DIM), small(HEAD_DIM), small(HEAD_DIM), small(HEAD_DIM),
                  small(hw)],
        out_specs=pl.BlockSpec((1, s, hw), lambda bi, h: (bi, 0, h)),
        out_shape=jax.ShapeDtypeStruct((b, s, n_heads * hw), jnp.bfloat16),
        scratch_shapes=[pltpu.VMEM((hw, s), jnp.bfloat16),
                        pltpu.VMEM((s, 2 * tq), jnp.float32)],
        compiler_params=_params(("arbitrary", "arbitrary")),
        name="diff_attn",
    )(qk3, qk3, v3, vec(lq1), vec(lk1), vec(lq2), vec(lk2), vec(gain))


def _out_proj_kernel(p_ref, a_ref, x_ref, w_ref, g_ref, x1_ref, h_ref, wbf_ref):
    @pl.when(pl.program_id(0) == 0)
    def _():
        wbf_ref[...] = w_ref[...].astype(jnp.bfloat16)

    kp = p_ref.shape[1]
    y = jnp.dot(p_ref[...], wbf_ref[:kp, :], preferred_element_type=jnp.float32)
    y += jnp.dot(a_ref[...], wbf_ref[kp:, :], preferred_element_type=jnp.float32)
    x1 = x_ref[...] + y
    x1_ref[...] = x1
    h_ref[...] = (_rms_scale(x1) * g_ref[...]).astype(h_ref.dtype)


def _out_proj(pool_out, attn_out, x2d, w_out, g, tm=512):
    m, d = x2d.shape
    kp, ka = pool_out.shape[1], attn_out.shape[1]
    return pl.pallas_call(
        _out_proj_kernel,
        grid=(m // tm,),
        in_specs=[pl.BlockSpec((tm, kp), lambda i: (i, 0)),
                  pl.BlockSpec((tm, ka), lambda i: (i, 0)),
                  pl.BlockSpec((tm, d), lambda i: (i, 0)),
                  pl.BlockSpec((kp + ka, d), lambda i: (0, 0),
                               pipeline_mode=pl.Buffered(1)),
                  pl.BlockSpec((1, d), lambda i: (0, 0))],
        out_specs=[pl.BlockSpec((tm, d), lambda i: (i, 0)),
                   pl.BlockSpec((tm, d), lambda i: (i, 0))],
        out_shape=[jax.ShapeDtypeStruct((m, d), jnp.float32),
                   jax.ShapeDtypeStruct((m, d), jnp.bfloat16)],
        scratch_shapes=[pltpu.VMEM((kp + ka, d), jnp.bfloat16)],
        compiler_params=_params(("arbitrary",)),
        name="out_proj",
    )(pool_out, attn_out, x2d, w_out, g.reshape(1, d))


def _gate_up_kernel(a_ref, wg_ref, wu_ref, wd_ref, o_ref, wdb_ref, wgb_ref, wub_ref):
    @pl.when(pl.program_id(1) == 0)
    def _():
        wgb_ref[...] = wg_ref[...].astype(jnp.bfloat16)
        wub_ref[...] = wu_ref[...].astype(jnp.bfloat16)

    wdb_ref[...] = wd_ref[...].astype(jnp.bfloat16)

    a = a_ref[...]
    gate = jnp.dot(a, wgb_ref[...], preferred_element_type=jnp.float32)
    up = jnp.dot(a, wub_ref[...], preferred_element_type=jnp.float32)
    o_ref[...] = (gate * jax.nn.sigmoid(gate) * up).astype(o_ref.dtype)


def _gate_up(a, w_gate_up, w_down, tm=1024, th=512):
    m, k = a.shape
    hidden, d = w_down.shape
    nb = hidden // th
    mb = m // tm
    slab, rem = divmod(hidden, nb * mb)
    assert rem == 0 and slab % 16 == 0, "w_down must split into bf16-tileable slabs per step"
    return pl.pallas_call(
        _gate_up_kernel,
        grid=(nb, mb),
        in_specs=[pl.BlockSpec((tm, k), lambda j, i: (i, 0)),
                  pl.BlockSpec((k, th), lambda j, i: (0, j)),
                  pl.BlockSpec((k, th), lambda j, i: (0, j + nb)),
                  pl.BlockSpec((slab, d), lambda j, i: (j * mb + i, 0))],
        out_specs=[pl.BlockSpec((tm, th), lambda j, i: (i, j)),
                   pl.BlockSpec((slab, d), lambda j, i: (j * mb + i, 0))],
        out_shape=[jax.ShapeDtypeStruct((m, hidden), jnp.bfloat16),
                   jax.ShapeDtypeStruct((hidden, d), jnp.bfloat16)],
        scratch_shapes=[pltpu.VMEM((k, th), jnp.bfloat16),
                        pltpu.VMEM((k, th), jnp.bfloat16)],
        compiler_params=_params(("arbitrary", "arbitrary")),
        name="gate_up",
    )(a, w_gate_up, w_gate_up, w_down)


def _down_kernel(a_ref, w_ref, x1_ref, g_ref, o_ref):
    o_ref[...] = x1_ref[...] + jnp.dot(a_ref[...], w_ref[...],
                                       preferred_element_type=jnp.float32)
    o_ref[...] = _rms_scale(o_ref[...]) * g_ref[...]


def _down(a, w_down_bf16, x1, g, tm=512):
    m, hidden = a.shape
    d = w_down_bf16.shape[1]
    return pl.pallas_call(
        _down_kernel,
        grid=(m // tm,),
        in_specs=[pl.BlockSpec((tm, hidden), lambda i: (i, 0)),
                  pl.BlockSpec((hidden, d), lambda i: (0, 0), pipeline_mode=pl.Buffered(1)),
                  pl.BlockSpec((tm, d), lambda i: (i, 0)),
                  pl.BlockSpec((1, d), lambda i: (0, 0))],
        out_specs=pl.BlockSpec((tm, d), lambda i: (i, 0)),
        out_shape=jax.ShapeDtypeStruct((m, d), jnp.float32),
        compiler_params=_params(("arbitrary",)),
        name="down",
    )(a, w_down_bf16, x1, g.reshape(1, d))


def _rotary_tables(seq):
    inv_freq = ROPE_THETA ** (-jnp.arange(0, HEAD_DIM, 2, dtype=jnp.float32) / HEAD_DIM)
    ang = jnp.arange(seq, dtype=jnp.float32)[:, None] * inv_freq[None, :]
    cos, sin = jnp.cos(ang), jnp.sin(ang)
    reps = HEAD_WIDTH // HEAD_DIM
    cos_t = jnp.tile(jnp.concatenate([cos, cos], axis=-1), (1, reps))
    sin_t = jnp.tile(jnp.concatenate([-sin, sin], axis=-1), (1, reps))
    return cos_t, sin_t


def kernel(x, norm_mix, w_in, w_pool, pool_scale, lambda_q1, lambda_k1, lambda_q2,
           lambda_k2, subln_gain, w_out, norm_ffn, w_gate_up, w_down, norm_final):
    b, s, d = x.shape
    depth = w_in.shape[0]
    assert depth == 1, "single-layer block"
    pool_width = w_pool.shape[1] * w_pool.shape[2]
    hidden = w_down.shape[1]
    attn_width = w_out.shape[1] - pool_width
    n_heads = attn_width // HEAD_WIDTH

    cos_t, sin_t = _rotary_tables(s)
    x2d = x.reshape(b * s, d)
    l = 0
    h1 = _prenorm(x2d, norm_mix[l])
    qk_width = n_heads * HEAD_WIDTH
    u = _in_proj(h1, w_in[l], col0=0, ncols=pool_width, out_dtype=jnp.float32,
                 name="in_proj_u")
    qk = _in_proj(h1, w_in[l], col0=pool_width, ncols=2 * qk_width, out_dtype=jnp.bfloat16,
                  name="in_proj_qk", rope=(cos_t, sin_t, qk_width))
    v = _in_proj(h1, w_in[l], col0=pool_width + 2 * qk_width, ncols=attn_width,
                 out_dtype=jnp.bfloat16, name="in_proj_v")
    pool_out = _pool(u.reshape(b, s, -1), w_pool[l], pool_scale[l])
    attn_out = _attention(qk.reshape(b, s, -1), v.reshape(b, s, -1), lambda_q1[l],
                          lambda_k1[l], lambda_q2[l], lambda_k2[l], subln_gain[l], n_heads)
    x1, h2 = _out_proj(pool_out.reshape(b * s, -1), attn_out.reshape(b * s, -1), x2d,
                       w_out[l], norm_ffn[l])
    a, w_down_bf16 = _gate_up(h2, w_gate_up[l], w_down[l])
    out = _down(a, w_down_bf16, x1, norm_final)
    return out.reshape(b, s, d)
```

```python
import functools
import math

import jax
import jax.numpy as jnp
from jax import lax
from jax.experimental import pallas as pl
from jax.experimental.pallas import tpu as pltpu

CHUNK = 64
POOL_WINDOWS = (2, 4, 8, 16)
HEAD_DIM = 64
HEAD_WIDTH = 2 * HEAD_DIM
ROPE_THETA = 10000.0
NORM_EPS = 1e-6
LAMBDA_INIT = 0.8 - 0.6 * math.exp(-0.3 * 0)

V7X_VMEM_LIMIT_BYTES = 56 * 1024 * 1024

BF16_SUBLANES = 16

_NEG_BIG = -1e30
LOG2_E = math.log2(math.e)


def _params(semantics):
    return pltpu.CompilerParams(dimension_semantics=semantics,
                                vmem_limit_bytes=V7X_VMEM_LIMIT_BYTES)


def _rms_scale(x):
    return x * lax.rsqrt(jnp.mean(x * x, axis=-1, keepdims=True) + NORM_EPS)


def _prenorm_kernel(x_ref, g_ref, o_ref):
    o_ref[...] = (_rms_scale(x_ref[...]) * g_ref[...]).astype(o_ref.dtype)


def _prenorm(x2d, g, tm=512):
    m, d = x2d.shape
    return pl.pallas_call(
        _prenorm_kernel,
        grid=(m // tm,),
        in_specs=[pl.BlockSpec((tm, d), lambda i: (i, 0)),
                  pl.BlockSpec((1, d), lambda i: (0, 0))],
        out_specs=pl.BlockSpec((tm, d), lambda i: (i, 0)),
        out_shape=jax.ShapeDtypeStruct((m, d), jnp.bfloat16),
        compiler_params=_params(("arbitrary",)),
        name="prenorm",
    )(x2d, g.reshape(1, d))


def _rope(t, cos_t, sin_t):
    width = t.shape[-1]
    half = HEAD_DIM // 2
    lane = lax.broadcasted_iota(jnp.int32, t.shape, 1)
    first_half = (lane % HEAD_DIM) < half
    partner = jnp.where(first_half,
                        pltpu.roll(t, width - half, axis=1),
                        pltpu.roll(t, half, axis=1))
    return t * cos_t + partner * sin_t


def _in_proj_kernel(a_ref, w_ref, o_ref, wbf_ref):
    @pl.when(pl.program_id(1) == 0)
    def _():
        wbf_ref[...] = w_ref[...].astype(jnp.bfloat16)

    o_ref[...] = jnp.dot(a_ref[...], wbf_ref[...],
                         preferred_element_type=jnp.float32).astype(o_ref.dtype)


def _in_proj_rope_kernel(a_ref, w_ref, cos_ref, sin_ref, o_ref, wbf_ref, *, n_q_tiles):
    @pl.when(pl.program_id(1) == 0)
    def _():
        wbf_ref[...] = w_ref[...].astype(jnp.bfloat16)

    y = jnp.dot(a_ref[...], wbf_ref[...], preferred_element_type=jnp.float32)
    scale = jnp.where(pl.program_id(0) < n_q_tiles, LOG2_E / math.sqrt(HEAD_DIM), 1.0)
    cos_t = cos_ref[...] * scale
    sin_t = sin_ref[...] * scale
    for h in range(y.shape[1] // HEAD_WIDTH):
        cols = slice(h * HEAD_WIDTH, (h + 1) * HEAD_WIDTH)
        o_ref[:, cols] = _rope(y[:, cols], cos_t, sin_t).astype(o_ref.dtype)


def _in_proj(a, w, *, col0, ncols, out_dtype, name, rope=None, tm=1024, tn=1024):
    m, k = a.shape
    assert col0 % tn == 0
    jb = col0 // tn
    in_specs = [pl.BlockSpec((tm, k), lambda j, i: (i, 0)),
                pl.BlockSpec((k, tn), lambda j, i: (0, j + jb))]
    args = (a, w)
    body = _in_proj_kernel
    if rope is not None:
        cos_t, sin_t, q_cols = rope
        seq = cos_t.shape[0]
        assert q_cols % tn == 0 and seq % tm == 0
        sb = seq // tm
        in_specs += [pl.BlockSpec((tm, HEAD_WIDTH), lambda j, i: (i % sb, 0)),
                     pl.BlockSpec((tm, HEAD_WIDTH), lambda j, i: (i % sb, 0))]
        args += (cos_t, sin_t)
        body = functools.partial(_in_proj_rope_kernel, n_q_tiles=q_cols // tn)
    return pl.pallas_call(
        body,
        grid=(ncols // tn, m // tm),
        in_specs=in_specs,
        out_specs=pl.BlockSpec((tm, tn), lambda j, i: (i, j)),
        out_shape=jax.ShapeDtypeStruct((m, ncols), out_dtype),
        scratch_shapes=[pltpu.VMEM((k, tn), jnp.bfloat16)],
        compiler_params=_params(("arbitrary", "arbitrary")),
        name=name,
    )(*args)


def _pool_kernel(u_ref, w_ref, s_ref, o_ref):
    g = pl.program_id(1)
    x = u_ref[0]
    seq = x.shape[0]
    row = lax.broadcasted_iota(jnp.int32, x.shape, 0)
    s = x
    for step in range(len(POOL_WINDOWS)):
        shift = 1 << step
        shifted = jnp.where(row >= shift, pltpu.roll(s, shift, axis=0), 0.0)
        s = jnp.where(g >= step, s + shifted, s)
    window = jnp.left_shift(2, g)
    cnt = jnp.minimum(row + 1, window).astype(jnp.float32)
    pooled = s / cnt - x
    mapped = jnp.dot(pooled.astype(jnp.bfloat16), w_ref[0].astype(jnp.bfloat16),
                     preferred_element_type=jnp.float32)
    o_ref[0] = (mapped * s_ref[...]).astype(o_ref.dtype)
    del seq


def _pool(proj3, w_pool, pool_scale):
    b, s, _ = proj3.shape
    ng, cg, _ = w_pool.shape
    return pl.pallas_call(
        _pool_kernel,
        grid=(b, ng),
        in_specs=[pl.BlockSpec((1, s, cg), lambda bi, g: (bi, 0, g)),
                  pl.BlockSpec((1, cg, cg), lambda bi, g: (g, 0, 0)),
                  pl.BlockSpec((1, cg), lambda bi, g: (0, g))],
        out_specs=pl.BlockSpec((1, s, cg), lambda bi, g: (bi, 0, g)),
        out_shape=jax.ShapeDtypeStruct((b, s, ng * cg), jnp.bfloat16),
        compiler_params=_params(("arbitrary", "arbitrary")),
        name="pool",
    )(proj3, w_pool, pool_scale.reshape(1, ng * cg))


def _attn_kernel(q_ref, k_ref, v_ref, lq1_ref, lk1_ref, lq2_ref, lk2_ref, gain_ref,
                 o_ref, vt_ref, st2_ref, *, tq):
    seq = q_ref.shape[1]
    n_tiles = seq // tq
    hw = HEAD_WIDTH

    vt_ref[:hw, :] = v_ref[0].astype(jnp.float32).T.astype(jnp.bfloat16)
    vt_ref[hw:, :] = jnp.ones((vt_ref.shape[0] - hw, seq), jnp.bfloat16)

    lam = (jnp.exp(jnp.sum(lq1_ref[...] * lk1_ref[...], axis=-1, keepdims=True))
           - jnp.exp(jnp.sum(lq2_ref[...] * lk2_ref[...], axis=-1, keepdims=True))
           + LAMBDA_INIT)

    lane = lax.broadcasted_iota(jnp.int32, (tq, hw), 1)
    key = lax.broadcasted_iota(jnp.int32, (tq, 2 * tq), 0)
    qry = lax.broadcasted_iota(jnp.int32, (tq, 2 * tq), 1) % tq
    diag_mask = (qry // CHUNK) >= (key // CHUNK)

    def masked_queries(i):
        q = q_ref[0, i * tq:(i + 1) * tq, :]
        return jnp.concatenate([jnp.where(lane < HEAD_DIM, q, 0),
                                jnp.where(lane >= HEAD_DIM, q, 0)], axis=0)

    def score_chunk(i, c, q2):
        rows = slice(c * tq, (c + 1) * tq)
        s = lax.dot_general(k_ref[0, rows, :], q2, (((1,), (1,)), ((), ())),
                            preferred_element_type=jnp.float32)
        if c == i:
            s = jnp.where(diag_mask, s, _NEG_BIG)
        st2_ref[i % 2, rows, :] = s
        return jnp.max(s, axis=0, keepdims=True)

    def value_chunk(i, c, m):
        rows = slice(c * tq, (c + 1) * tq)
        p = jnp.exp2(st2_ref[i % 2, rows, :] - m).astype(jnp.bfloat16)
        return jnp.dot(vt_ref[:, rows], p, preferred_element_type=jnp.float32)

    def running_max(m, mc):
        return mc if m is None else jnp.maximum(m, mc)

    q2_next = masked_queries(0)
    m_next = score_chunk(0, 0, q2_next)
    for i in range(n_tiles):
        m_cur, m_next = m_next, None
        has_next = i + 1 < n_tiles
        if has_next:
            q2_next = masked_queries(i + 1)
        acc = None
        for c in range(i + 2):
            if has_next:
                m_next = running_max(m_next, score_chunk(i + 1, c, q2_next))
            if c <= i:
                part = value_chunk(i, c, m_cur)
                acc = part if acc is None else acc + part
        ot = acc[:hw] * (1.0 / acc[hw:hw + 1])
        o = (ot[:, :tq] - lam * ot[:, tq:]).T
        o = _rms_scale(o) * gain_ref[...] * (1.0 - LAMBDA_INIT)
        o_ref[0, i * tq:(i + 1) * tq, :] = o.astype(o_ref.dtype)


def _attention(qk3, v3, lq1, lk1, lq2, lk2, gain, n_heads, tq=256):
    b, s, _ = v3.shape
    hw = HEAD_WIDTH
    vec = lambda a: a.reshape(1, -1)
    small = lambda n: pl.BlockSpec((1, n), lambda bi, h: (0, 0))
    return pl.pallas_call(
        functools.partial(_attn_kernel, tq=tq),
        grid=(b, n_heads),
        in_specs=[pl.BlockSpec((1, s, hw), lambda bi, h: (bi, 0, h)),
                  pl.BlockSpec((1, s, hw), lambda bi, h: (bi, 0, n_heads + h)),
                  pl.BlockSpec((1, s, hw), lambda bi, h: (bi, 0, h)),
                  small(HEAD_DIM), small(HEAD_DIM), small(HEAD_DIM), small(HEAD_DIM),
                  small(hw)],
        out_specs=pl.BlockSpec((1, s, hw), lambda bi, h: (bi, 0, h)),
        out_shape=jax.ShapeDtypeStruct((b, s, n_heads * hw), jnp.bfloat16),
        scratch_shapes=[pltpu.VMEM((hw + BF16_SUBLANES, s), jnp.bfloat16),
                        pltpu.VMEM((2, s, 2 * tq), jnp.float32)],
        compiler_params=_params(("arbitrary", "arbitrary")),
        name="diff_attn",
    )(qk3, qk3, v3, vec(lq1), vec(lk1), vec(lq2), vec(lk2), vec(gain))


def _out_proj_kernel(p_ref, a_ref, x_ref, w_ref, g_ref, x1_ref, h_ref, wbf_ref):
    @pl.when(pl.program_id(0) == 0)
    def _():
        wbf_ref[...] = w_ref[...].astype(jnp.bfloat16)

    kp = p_ref.shape[1]
    y = jnp.dot(p_ref[...], wbf_ref[:kp, :], preferred_element_type=jnp.float32)
    y += jnp.dot(a_ref[...], wbf_ref[kp:, :], preferred_element_type=jnp.float32)
    x1 = x_ref[...] + y
    x1_ref[...] = x1
    h_ref[...] = (_rms_scale(x1) * g_ref[...]).astype(h_ref.dtype)


def _out_proj(pool_out, attn_out, x2d, w_out, g, tm=512):
    m, d = x2d.shape
    kp, ka = pool_out.shape[1], attn_out.shape[1]
    return pl.pallas_call(
        _out_proj_kernel,
        grid=(m // tm,),
        in_specs=[pl.BlockSpec((tm, kp), lambda i: (i, 0)),
                  pl.BlockSpec((tm, ka), lambda i: (i, 0)),
                  pl.BlockSpec((tm, d), lambda i: (i, 0)),
                  pl.BlockSpec((kp + ka, d), lambda i: (0, 0),
                               pipeline_mode=pl.Buffered(1)),
                  pl.BlockSpec((1, d), lambda i: (0, 0))],
        out_specs=[pl.BlockSpec((tm, d), lambda i: (i, 0)),
                   pl.BlockSpec((tm, d), lambda i: (i, 0))],
        out_shape=[jax.ShapeDtypeStruct((m, d), jnp.float32),
                   jax.ShapeDtypeStruct((m, d), jnp.bfloat16)],
        scratch_shapes=[pltpu.VMEM((kp + ka, d), jnp.bfloat16)],
        compiler_params=_params(("arbitrary",)),
        name="out_proj",
    )(pool_out, attn_out, x2d, w_out, g.reshape(1, d))


def _gate_up_kernel(a_ref, wg_ref, wu_ref, wd_ref, o_ref, wdb_ref, wgb_ref, wub_ref):
    @pl.when(pl.program_id(1) == 0)
    def _():
        wgb_ref[...] = wg_ref[...].astype(jnp.bfloat16)
        wub_ref[...] = wu_ref[...].astype(jnp.bfloat16)

    wdb_ref[...] = wd_ref[...].astype(jnp.bfloat16)

    a = a_ref[...]
    gate = jnp.dot(a, wgb_ref[...], preferred_element_type=jnp.float32)
    up = jnp.dot(a, wub_ref[...], preferred_element_type=jnp.float32)
    o_ref[...] = (gate * jax.nn.sigmoid(gate) * up).astype(o_ref.dtype)


def _gate_up(a, w_gate_up, w_down, tm=1024, th=512):
    m, k = a.shape
    hidden, d = w_down.shape
    nb = hidden // th
    mb = m // tm
    slab, rem = divmod(hidden, nb * mb)
    assert rem == 0 and slab % 16 == 0, "w_down must split into bf16-tileable slabs per step"
    return pl.pallas_call(
        _gate_up_kernel,
        grid=(nb, mb),
        in_specs=[pl.BlockSpec((tm, k), lambda j, i: (i, 0)),
                  pl.BlockSpec((k, th), lambda j, i: (0, j)),
                  pl.BlockSpec((k, th), lambda j, i: (0, j + nb)),
                  pl.BlockSpec((slab, d), lambda j, i: (j * mb + i, 0))],
        out_specs=[pl.BlockSpec((tm, th), lambda j, i: (i, j)),
                   pl.BlockSpec((slab, d), lambda j, i: (j * mb + i, 0))],
        out_shape=[jax.ShapeDtypeStruct((m, hidden), jnp.bfloat16),
                   jax.ShapeDtypeStruct((hidden, d), jnp.bfloat16)],
        scratch_shapes=[pltpu.VMEM((k, th), jnp.bfloat16),
                        pltpu.VMEM((k, th), jnp.bfloat16)],
        compiler_params=_params(("arbitrary", "arbitrary")),
        name="gate_up",
    )(a, w_gate_up, w_gate_up, w_down)


def _down_kernel(a_ref, w_ref, x1_ref, g_ref, o_ref):
    o_ref[...] = x1_ref[...] + jnp.dot(a_ref[...], w_ref[...],
                                       preferred_element_type=jnp.float32)
    o_ref[...] = _rms_scale(o_ref[...]) * g_ref[...]


def _down(a, w_down_bf16, x1, g, tm=512):
    m, hidden = a.shape
    d = w_down_bf16.shape[1]
    return pl.pallas_call(
        _down_kernel,
        grid=(m // tm,),
        in_specs=[pl.BlockSpec((tm, hidden), lambda i: (i, 0)),
                  pl.BlockSpec((hidden, d), lambda i: (0, 0), pipeline_mode=pl.Buffered(1)),
                  pl.BlockSpec((tm, d), lambda i: (i, 0)),
                  pl.BlockSpec((1, d), lambda i: (0, 0))],
        out_specs=pl.BlockSpec((tm, d), lambda i: (i, 0)),
        out_shape=jax.ShapeDtypeStruct((m, d), jnp.float32),
        compiler_params=_params(("arbitrary",)),
        name="down",
    )(a, w_down_bf16, x1, g.reshape(1, d))


def _rotary_tables(seq):
    inv_freq = ROPE_THETA ** (-jnp.arange(0, HEAD_DIM, 2, dtype=jnp.float32) / HEAD_DIM)
    ang = jnp.arange(seq, dtype=jnp.float32)[:, None] * inv_freq[None, :]
    cos, sin = jnp.cos(ang), jnp.sin(ang)
    reps = HEAD_WIDTH // HEAD_DIM
    cos_t = jnp.tile(jnp.concatenate([cos, cos], axis=-1), (1, reps))
    sin_t = jnp.tile(jnp.concatenate([-sin, sin], axis=-1), (1, reps))
    return cos_t, sin_t


def kernel(x, norm_mix, w_in, w_pool, pool_scale, lambda_q1, lambda_k1, lambda_q2,
           lambda_k2, subln_gain, w_out, norm_ffn, w_gate_up, w_down, norm_final):
    b, s, d = x.shape
    depth = w_in.shape[0]
    assert depth == 1, "single-layer block"
    pool_width = w_pool.shape[1] * w_pool.shape[2]
    hidden = w_down.shape[1]
    attn_width = w_out.shape[1] - pool_width
    n_heads = attn_width // HEAD_WIDTH

    cos_t, sin_t = _rotary_tables(s)
    x2d = x.reshape(b * s, d)
    l = 0
    h1 = _prenorm(x2d, norm_mix[l])
    qk_width = n_heads * HEAD_WIDTH
    u = _in_proj(h1, w_in[l], col0=0, ncols=pool_width, out_dtype=jnp.float32,
                 name="in_proj_u")
    qk = _in_proj(h1, w_in[l], col0=pool_width, ncols=2 * qk_width, out_dtype=jnp.bfloat16,
                  name="in_proj_qk", rope=(cos_t, sin_t, qk_width))
    v = _in_proj(h1, w_in[l], col0=pool_width + 2 * qk_width, ncols=attn_width,
                 out_dtype=jnp.bfloat16, name="in_proj_v")
    pool_out = _pool(u.reshape(b, s, -1), w_pool[l], pool_scale[l])
    attn_out = _attention(qk.reshape(b, s, -1), v.reshape(b, s, -1), lambda_q1[l],
                          lambda_k1[l], lambda_q2[l], lambda_k2[l], subln_gain[l], n_heads)
    x1, h2 = _out_proj(pool_out.reshape(b * s, -1), attn_out.reshape(b * s, -1), x2d,
                       w_out[l], norm_ffn[l])
    a, w_down_bf16 = _gate_up(h2, w_gate_up[l], w_down[l])
    out = _down(a, w_down_bf16, x1, norm_final)
    return out.reshape(b, s, d)
```

```python
import functools
import math

import jax
import jax.numpy as jnp
from jax import lax
from jax.experimental import pallas as pl
from jax.experimental.pallas import tpu as pltpu

CHUNK = 64
POOL_WINDOWS = (2, 4, 8, 16)
HEAD_DIM = 64
HEAD_WIDTH = 2 * HEAD_DIM
ROPE_THETA = 10000.0
NORM_EPS = 1e-6
LAMBDA_INIT = 0.8 - 0.6 * math.exp(-0.3 * 0)

V7X_VMEM_LIMIT_BYTES = 56 * 1024 * 1024

F32_SUBLANES = 8
BF16_SUBLANES = 16

_NEG_BIG = -1e30
LOG2_E = math.log2(math.e)


def _params(semantics):
    return pltpu.CompilerParams(dimension_semantics=semantics,
                                vmem_limit_bytes=V7X_VMEM_LIMIT_BYTES)


def _rms_scale(x):
    return x * lax.rsqrt(jnp.mean(x * x, axis=-1, keepdims=True) + NORM_EPS)


def _norm_proj_kernel(x_ref, g_ref, w_ref, h_ref, o_ref, wbf_ref):
    @pl.when(pl.program_id(0) == 0)
    def _():
        wbf_ref[...] = w_ref[...].astype(jnp.bfloat16)

    h = (_rms_scale(x_ref[...]) * g_ref[...]).astype(jnp.bfloat16)
    h_ref[...] = h
    o_ref[...] = jnp.dot(h, wbf_ref[...], preferred_element_type=jnp.float32)


def _norm_proj(x2d, g, w, ncols, tm=512):
    m, d = x2d.shape
    return pl.pallas_call(
        _norm_proj_kernel,
        grid=(m // tm,),
        in_specs=[pl.BlockSpec((tm, d), lambda i: (i, 0)),
                  pl.BlockSpec((1, d), lambda i: (0, 0)),
                  pl.BlockSpec((d, ncols), lambda i: (0, 0), pipeline_mode=pl.Buffered(1))],
        out_specs=[pl.BlockSpec((tm, d), lambda i: (i, 0)),
                   pl.BlockSpec((tm, ncols), lambda i: (i, 0))],
        out_shape=[jax.ShapeDtypeStruct((m, d), jnp.bfloat16),
                   jax.ShapeDtypeStruct((m, ncols), jnp.float32)],
        scratch_shapes=[pltpu.VMEM((d, ncols), jnp.bfloat16)],
        compiler_params=_params(("arbitrary",)),
        name="norm_proj_u",
    )(x2d, g.reshape(1, d), w)


def _rope(t, cos_t, sin_t):
    width = t.shape[-1]
    half = HEAD_DIM // 2
    lane = lax.broadcasted_iota(jnp.int32, t.shape, 1)
    first_half = (lane % HEAD_DIM) < half
    partner = jnp.where(first_half,
                        pltpu.roll(t, width - half, axis=1),
                        pltpu.roll(t, half, axis=1))
    return t * cos_t + partner * sin_t


def _in_proj_kernel(a_ref, w_ref, src_ref, o_ref, dst_ref, wbf_ref):
    @pl.when(pl.program_id(1) == 0)
    def _():
        wbf_ref[...] = w_ref[...].astype(jnp.bfloat16)

    dst_ref[...] = src_ref[...].astype(jnp.bfloat16)

    o_ref[...] = jnp.dot(a_ref[...], wbf_ref[...],
                         preferred_element_type=jnp.float32).astype(o_ref.dtype)


def _in_proj_rope_kernel(a_ref, w_ref, cos_ref, sin_ref, o_ref, wbf_ref, *, n_q_tiles):
    @pl.when(pl.program_id(1) == 0)
    def _():
        wbf_ref[...] = w_ref[...].astype(jnp.bfloat16)

    y = jnp.dot(a_ref[...], wbf_ref[...], preferred_element_type=jnp.float32)
    scale = jnp.where(pl.program_id(0) < n_q_tiles, LOG2_E / math.sqrt(HEAD_DIM), 1.0)
    cos_t = cos_ref[...] * scale
    sin_t = sin_ref[...] * scale
    for h in range(y.shape[1] // HEAD_WIDTH):
        cols = slice(h * HEAD_WIDTH, (h + 1) * HEAD_WIDTH)
        o_ref[:, cols] = _rope(y[:, cols], cos_t, sin_t).astype(o_ref.dtype)


def _in_proj_rope(a, w, cos_t, sin_t, *, col0, ncols, q_cols, tm=1024, tn=1024):
    m, k = a.shape
    seq = cos_t.shape[0]
    assert col0 % tn == 0 and q_cols % tn == 0 and seq % tm == 0
    jb = col0 // tn
    sb = seq // tm
    return pl.pallas_call(
        functools.partial(_in_proj_rope_kernel, n_q_tiles=q_cols // tn),
        grid=(ncols // tn, m // tm),
        in_specs=[pl.BlockSpec((tm, k), lambda j, i: (i, 0)),
                  pl.BlockSpec((k, tn), lambda j, i: (0, j + jb)),
                  pl.BlockSpec((tm, HEAD_WIDTH), lambda j, i: (i % sb, 0)),
                  pl.BlockSpec((tm, HEAD_WIDTH), lambda j, i: (i % sb, 0))],
        out_specs=pl.BlockSpec((tm, tn), lambda j, i: (i, j)),
        out_shape=jax.ShapeDtypeStruct((m, ncols), jnp.bfloat16),
        scratch_shapes=[pltpu.VMEM((k, tn), jnp.bfloat16)],
        compiler_params=_params(("arbitrary", "arbitrary")),
        name="in_proj_qk",
    )(a, w, cos_t, sin_t)


def _in_proj_convert(a, w, src, *, col0, ncols, tm=1024, tn=1024):
    m, k = a.shape
    rows, d = src.shape
    assert col0 % tn == 0
    jb = col0 // tn
    nb, mb = ncols // tn, m // tm
    slab, rem = divmod(rows, nb * mb)
    assert rem == 0 and slab % BF16_SUBLANES == 0
    return pl.pallas_call(
        _in_proj_kernel,
        grid=(nb, mb),
        in_specs=[pl.BlockSpec((tm, k), lambda j, i: (i, 0)),
                  pl.BlockSpec((k, tn), lambda j, i: (0, j + jb)),
                  pl.BlockSpec((slab, d), lambda j, i: (j * mb + i, 0))],
        out_specs=[pl.BlockSpec((tm, tn), lambda j, i: (i, j)),
                   pl.BlockSpec((slab, d), lambda j, i: (j * mb + i, 0))],
        out_shape=[jax.ShapeDtypeStruct((m, ncols), jnp.bfloat16),
                   jax.ShapeDtypeStruct((rows, d), jnp.bfloat16)],
        scratch_shapes=[pltpu.VMEM((k, tn), jnp.bfloat16)],
        compiler_params=_params(("arbitrary", "arbitrary")),
        name="in_proj_v",
    )(a, w, src)


def _shift_rows_down(s, shift):
    width = s.shape[1]
    if shift == F32_SUBLANES:
        return jnp.concatenate([jnp.zeros((shift, width), s.dtype), s[:-shift]], axis=0)
    rolled = pltpu.roll(s, shift, axis=0)
    row = lax.broadcasted_iota(jnp.int32, (F32_SUBLANES, width), 0)
    top = jnp.where(row >= shift, rolled[:F32_SUBLANES], 0.0)
    return jnp.concatenate([top, rolled[F32_SUBLANES:]], axis=0)


def _pool_group(x, window):
    s = x
    shift = 1
    while shift < window:
        s = s + _shift_rows_down(s, shift)
        shift *= 2
    pooled = s * (1.0 / window) - x
    head = max(F32_SUBLANES, window)
    cnt = (lax.broadcasted_iota(jnp.int32, (head, x.shape[1]), 0) + 1).astype(jnp.float32)
    cnt = jnp.minimum(cnt, float(window))
    top = s[:head] / cnt - x[:head]
    return jnp.concatenate([top, pooled[head:]], axis=0)


def _pool_kernel(u_ref, w_ref, s_ref, o_ref):
    g = pl.program_id(1)
    wbf = w_ref[0].astype(jnp.bfloat16)
    for gi, window in enumerate(POOL_WINDOWS):
        @pl.when(g == gi)
        def _(window=window):
            pooled = _pool_group(u_ref[0], window)
            mapped = jnp.dot(pooled.astype(jnp.bfloat16), wbf,
                             preferred_element_type=jnp.float32)
            o_ref[0] = (mapped * s_ref[...]).astype(o_ref.dtype)


def _pool(proj3, w_pool, pool_scale):
    b, s, _ = proj3.shape
    ng, cg, _ = w_pool.shape
    return pl.pallas_call(
        _pool_kernel,
        grid=(b, ng),
        in_specs=[pl.BlockSpec((1, s, cg), lambda bi, g: (bi, 0, g)),
                  pl.BlockSpec((1, cg, cg), lambda bi, g: (g, 0, 0)),
                  pl.BlockSpec((1, cg), lambda bi, g: (0, g))],
        out_specs=pl.BlockSpec((1, s, cg), lambda bi, g: (bi, 0, g)),
        out_shape=jax.ShapeDtypeStruct((b, s, ng * cg), jnp.bfloat16),
        compiler_params=_params(("arbitrary", "arbitrary")),
        name="pool",
    )(proj3, w_pool, pool_scale.reshape(1, ng * cg))


def _attn_kernel(q_ref, k_ref, v_ref, lq1_ref, lk1_ref, lq2_ref, lk2_ref, gain_ref,
                 o_ref, vt_ref, st2_ref, *, tq):
    seq = q_ref.shape[1]
    n_tiles = seq // tq
    hw = HEAD_WIDTH

    vt_ref[:hw, :] = v_ref[0].astype(jnp.float32).T.astype(jnp.bfloat16)
    vt_ref[hw:, :] = jnp.ones((vt_ref.shape[0] - hw, seq), jnp.bfloat16)

    lam = (jnp.exp(jnp.sum(lq1_ref[...] * lk1_ref[...], axis=-1, keepdims=True))
           - jnp.exp(jnp.sum(lq2_ref[...] * lk2_ref[...], axis=-1, keepdims=True))
           + LAMBDA_INIT)

    lane = lax.broadcasted_iota(jnp.int32, (tq, hw), 1)
    key = lax.broadcasted_iota(jnp.int32, (tq, 2 * tq), 0)
    qry = lax.broadcasted_iota(jnp.int32, (tq, 2 * tq), 1) % tq
    diag_mask = (qry // CHUNK) >= (key // CHUNK)

    def masked_queries(i):
        q = q_ref[0, i * tq:(i + 1) * tq, :]
        return jnp.concatenate([jnp.where(lane < HEAD_DIM, q, 0),
                                jnp.where(lane >= HEAD_DIM, q, 0)], axis=0)

    def score_chunk(i, c, q2):
        rows = slice(c * tq, (c + 1) * tq)
        s = lax.dot_general(k_ref[0, rows, :], q2, (((1,), (1,)), ((), ())),
                            preferred_element_type=jnp.float32)
        if c == i:
            s = jnp.where(diag_mask, s, _NEG_BIG)
        st2_ref[i % 2, rows, :] = s
        return jnp.max(s, axis=0, keepdims=True)

    def value_chunk(i, c, m):
        rows = slice(c * tq, (c + 1) * tq)
        p = jnp.exp2(st2_ref[i % 2, rows, :] - m).astype(jnp.bfloat16)
        return jnp.dot(vt_ref[:, rows], p, preferred_element_type=jnp.float32)

    def running_max(m, mc):
        return mc if m is None else jnp.maximum(m, mc)

    q2_next = masked_queries(0)
    m_next = score_chunk(0, 0, q2_next)
    for i in range(n_tiles):
        m_cur, m_next = m_next, None
        has_next = i + 1 < n_tiles
        if has_next:
            q2_next = masked_queries(i + 1)
        acc = None
        for c in range(i + 2):
            if has_next:
                m_next = running_max(m_next, score_chunk(i + 1, c, q2_next))
            if c <= i:
                part = value_chunk(i, c, m_cur)
                acc = part if acc is None else acc + part
        ot = acc[:hw] * (1.0 / acc[hw:hw + 1])
        o = (ot[:, :tq] - lam * ot[:, tq:]).T
        o = _rms_scale(o) * gain_ref[...] * (1.0 - LAMBDA_INIT)
        o_ref[0, i * tq:(i + 1) * tq, :] = o.astype(o_ref.dtype)


def _attention(qk3, v3, lq1, lk1, lq2, lk2, gain, n_heads, tq=256):
    b, s, _ = v3.shape
    hw = HEAD_WIDTH
    vec = lambda a: a.reshape(1, -1)
    small = lambda n: pl.BlockSpec((1, n), lambda bi, h: (0, 0))
    return pl.pallas_call(
        functools.partial(_attn_kernel, tq=tq),
        grid=(b, n_heads),
        in_specs=[pl.BlockSpec((1, s, hw), lambda bi, h: (bi, 0, h)),
                  pl.BlockSpec((1, s, hw), lambda bi, h: (bi, 0, n_heads + h)),
                  pl.BlockSpec((1, s, hw), lambda bi, h: (bi, 0, h)),
                  small(HEAD_DIM), small(HEAD_DIM), small(HEAD_DIM), small(HEAD_DIM),
                  small(hw)],
        out_specs=pl.BlockSpec((1, s, hw), lambda bi, h: (bi, 0, h)),
        out_shape=jax.ShapeDtypeStruct((b, s, n_heads * hw), jnp.bfloat16),
        scratch_shapes=[pltpu.VMEM((hw + BF16_SUBLANES, s), jnp.bfloat16),
                        pltpu.VMEM((2, s, 2 * tq), jnp.float32)],
        compiler_params=_params(("arbitrary", "arbitrary")),
        name="diff_attn",
    )(qk3, qk3, v3, vec(lq1), vec(lk1), vec(lq2), vec(lk2), vec(gain))


def _out_proj_kernel(p_ref, a_ref, x_ref, w_ref, g_ref, x1_ref, h_ref):
    kp = p_ref.shape[1]
    y = jnp.dot(p_ref[...], w_ref[:kp, :], preferred_element_type=jnp.float32)
    y += jnp.dot(a_ref[...], w_ref[kp:, :], preferred_element_type=jnp.float32)
    x1_ref[...] = x_ref[...] + y
    h_ref[...] = (_rms_scale(x1_ref[...]) * g_ref[...]).astype(h_ref.dtype)


def _out_proj(pool_out, attn_out, x2d, w_out_bf16, g, tm=512):
    m, d = x2d.shape
    kp, ka = pool_out.shape[1], attn_out.shape[1]
    return pl.pallas_call(
        _out_proj_kernel,
        grid=(m // tm,),
        in_specs=[pl.BlockSpec((tm, kp), lambda i: (i, 0)),
                  pl.BlockSpec((tm, ka), lambda i: (i, 0)),
                  pl.BlockSpec((tm, d), lambda i: (i, 0)),
                  pl.BlockSpec((kp + ka, d), lambda i: (0, 0),
                               pipeline_mode=pl.Buffered(1)),
                  pl.BlockSpec((1, d), lambda i: (0, 0))],
        out_specs=[pl.BlockSpec((tm, d), lambda i: (i, 0)),
                   pl.BlockSpec((tm, d), lambda i: (i, 0))],
        out_shape=[jax.ShapeDtypeStruct((m, d), jnp.float32),
                   jax.ShapeDtypeStruct((m, d), jnp.bfloat16)],
        compiler_params=_params(("arbitrary",)),
        name="out_proj",
    )(pool_out, attn_out, x2d, w_out_bf16, g.reshape(1, d))


def _gate_up_kernel(a_ref, wg_ref, wu_ref, wd_ref, o_ref, wdb_ref, wgb_ref, wub_ref):
    @pl.when(pl.program_id(1) == 0)
    def _():
        wgb_ref[...] = wg_ref[...].astype(jnp.bfloat16)
        wub_ref[...] = wu_ref[...].astype(jnp.bfloat16)

    wdb_ref[...] = wd_ref[...].astype(jnp.bfloat16)

    a = a_ref[...]
    gate = jnp.dot(a, wgb_ref[...], preferred_element_type=jnp.float32)
    up = jnp.dot(a, wub_ref[...], preferred_element_type=jnp.float32)
    o_ref[...] = (gate * jax.nn.sigmoid(gate) * up).astype(o_ref.dtype)


def _gate_up(a, w_gate_up, w_down, tm=1024, th=512):
    m, k = a.shape
    hidden, d = w_down.shape
    nb = hidden // th
    mb = m // tm
    slab, rem = divmod(hidden, nb * mb)
    assert rem == 0 and slab % 16 == 0, "w_down must split into bf16-tileable slabs per step"
    return pl.pallas_call(
        _gate_up_kernel,
        grid=(nb, mb),
        in_specs=[pl.BlockSpec((tm, k), lambda j, i: (i, 0)),
                  pl.BlockSpec((k, th), lambda j, i: (0, j)),
                  pl.BlockSpec((k, th), lambda j, i: (0, j + nb)),
                  pl.BlockSpec((slab, d), lambda j, i: (j * mb + i, 0))],
        out_specs=[pl.BlockSpec((tm, th), lambda j, i: (i, j)),
                   pl.BlockSpec((slab, d), lambda j, i: (j * mb + i, 0))],
        out_shape=[jax.ShapeDtypeStruct((m, hidden), jnp.bfloat16),
                   jax.ShapeDtypeStruct((hidden, d), jnp.bfloat16)],
        scratch_shapes=[pltpu.VMEM((k, th), jnp.bfloat16),
                        pltpu.VMEM((k, th), jnp.bfloat16)],
        compiler_params=_params(("arbitrary", "arbitrary")),
        name="gate_up",
    )(a, w_gate_up, w_gate_up, w_down)


def _down_kernel(a_ref, w_ref, x1_ref, g_ref, o_ref):
    o_ref[...] = x1_ref[...] + jnp.dot(a_ref[...], w_ref[...],
                                       preferred_element_type=jnp.float32)
    o_ref[...] = _rms_scale(o_ref[...]) * g_ref[...]


def _down(a, w_down_bf16, x1, g, tm=512):
    m, hidden = a.shape
    d = w_down_bf16.shape[1]
    return pl.pallas_call(
        _down_kernel,
        grid=(m // tm,),
        in_specs=[pl.BlockSpec((tm, hidden), lambda i: (i, 0)),
                  pl.BlockSpec((hidden, d), lambda i: (0, 0), pipeline_mode=pl.Buffered(1)),
                  pl.BlockSpec((tm, d), lambda i: (i, 0)),
                  pl.BlockSpec((1, d), lambda i: (0, 0))],
        out_specs=pl.BlockSpec((tm, d), lambda i: (i, 0)),
        out_shape=jax.ShapeDtypeStruct((m, d), jnp.float32),
        compiler_params=_params(("arbitrary",)),
        name="down",
    )(a, w_down_bf16, x1, g.reshape(1, d))


def _rotary_tables(seq):
    inv_freq = ROPE_THETA ** (-jnp.arange(0, HEAD_DIM, 2, dtype=jnp.float32) / HEAD_DIM)
    ang = jnp.arange(seq, dtype=jnp.float32)[:, None] * inv_freq[None, :]
    cos, sin = jnp.cos(ang), jnp.sin(ang)
    reps = HEAD_WIDTH // HEAD_DIM
    cos_t = jnp.tile(jnp.concatenate([cos, cos], axis=-1), (1, reps))
    sin_t = jnp.tile(jnp.concatenate([-sin, sin], axis=-1), (1, reps))
    return cos_t, sin_t


def kernel(x, norm_mix, w_in, w_pool, pool_scale, lambda_q1, lambda_k1, lambda_q2,
           lambda_k2, subln_gain, w_out, norm_ffn, w_gate_up, w_down, norm_final):
    b, s, d = x.shape
    depth = w_in.shape[0]
    assert depth == 1, "single-layer block"
    pool_width = w_pool.shape[1] * w_pool.shape[2]
    attn_width = w_out.shape[1] - pool_width
    n_heads = attn_width // HEAD_WIDTH

    cos_t, sin_t = _rotary_tables(s)
    x2d = x.reshape(b * s, d)
    l = 0
    h1, u = _norm_proj(x2d, norm_mix[l], w_in[l], pool_width)
    qk_width = n_heads * HEAD_WIDTH
    qk = _in_proj_rope(h1, w_in[l], cos_t, sin_t, col0=pool_width, ncols=2 * qk_width,
                       q_cols=qk_width)
    v, w_out_bf16 = _in_proj_convert(h1, w_in[l], w_out[l], col0=pool_width + 2 * qk_width,
                                     ncols=attn_width)
    pool_out = _pool(u.reshape(b, s, -1), w_pool[l], pool_scale[l])
    attn_out = _attention(qk.reshape(b, s, -1), v.reshape(b, s, -1), lambda_q1[l],
                          lambda_k1[l], lambda_q2[l], lambda_k2[l], subln_gain[l], n_heads)
    x1, h2 = _out_proj(pool_out.reshape(b * s, -1), attn_out.reshape(b * s, -1), x2d,
                       w_out_bf16, norm_ffn[l])
    a, w_down_bf16 = _gate_up(h2, w_gate_up[l], w_down[l])
    out = _down(a, w_down_bf16, x1, norm_final)
    return out.reshape(b, s, d)
```

```python
import functools
import math

import jax
import jax.numpy as jnp
from jax import lax
from jax.experimental import pallas as pl
from jax.experimental.pallas import tpu as pltpu

CHUNK = 64
POOL_WINDOWS = (2, 4, 8, 16)
HEAD_DIM = 64
HEAD_WIDTH = 2 * HEAD_DIM
ROPE_THETA = 10000.0
NORM_EPS = 1e-6
LAMBDA_INIT = 0.8 - 0.6 * math.exp(-0.3 * 0)

V7X_VMEM_LIMIT_BYTES = 56 * 1024 * 1024

F32_SUBLANES = 8
BF16_SUBLANES = 16

_NEG_BIG = -1e30
LOG2_E = math.log2(math.e)


def _params(semantics):
    return pltpu.CompilerParams(dimension_semantics=semantics,
                                vmem_limit_bytes=V7X_VMEM_LIMIT_BYTES)


def _row_subtiles(rows, n=2):
    step, rem = divmod(rows, n)
    assert rem == 0
    return [slice(r * step, (r + 1) * step) for r in range(n)]


def _rms_scale(x):
    return x * lax.rsqrt(jnp.mean(x * x, axis=-1, keepdims=True) + NORM_EPS)


POOL_HALO = 16


def _shift_rows_down(s, shift):
    if shift == F32_SUBLANES:
        return jnp.concatenate([s[:shift], s[:-shift]], axis=0)
    return pltpu.roll(s, shift, axis=0)


def _pool_group(prev, u, window, is_seq_start):
    s = jnp.concatenate([prev, u], axis=0)
    shift = 1
    while shift < window:
        s = s + _shift_rows_down(s, shift)
        shift *= 2
    s = s[POOL_HALO:]
    pooled = s * (1.0 / window) - u
    cnt = (lax.broadcasted_iota(jnp.int32, (POOL_HALO, u.shape[1]), 0) + 1).astype(jnp.float32)
    cnt = jnp.where(is_seq_start, jnp.minimum(cnt, float(window)), float(window))
    top = s[:POOL_HALO] / cnt - u[:POOL_HALO]
    return jnp.concatenate([top, pooled[POOL_HALO:]], axis=0)


def _norm_pool_kernel(x_ref, g_ref, w_ref, wp_ref, ps_ref, h_ref, o_ref,
                      wbf_ref, wpb_ref, halo_ref, *, tiles_per_seq):
    i = pl.program_id(0)

    @pl.when(i == 0)
    def _():
        wbf_ref[...] = w_ref[...].astype(jnp.bfloat16)
        wpb_ref[...] = wp_ref[...].astype(jnp.bfloat16)
        halo_ref[...] = jnp.zeros_like(halo_ref)

    h = (_rms_scale(x_ref[...]) * g_ref[...]).astype(jnp.bfloat16)
    h_ref[...] = h
    u = jnp.dot(h, wbf_ref[...], preferred_element_type=jnp.float32)

    is_seq_start = (i % tiles_per_seq) == 0
    prev = jnp.where(is_seq_start, 0.0, halo_ref[...])
    halo_ref[...] = u[-POOL_HALO:]
    cg = wp_ref.shape[1]
    for gi, window in enumerate(POOL_WINDOWS):
        cols = slice(gi * cg, (gi + 1) * cg)
        pooled = _pool_group(prev[:, cols], u[:, cols], window, is_seq_start)
        mapped = jnp.dot(pooled.astype(jnp.bfloat16), wpb_ref[gi],
                         preferred_element_type=jnp.float32)
        o_ref[:, cols] = (mapped * ps_ref[:, cols]).astype(o_ref.dtype)


def _norm_pool(x2d, g, w, w_pool, pool_scale, seq, tm=512):
    m, d = x2d.shape
    ng, cg, _ = w_pool.shape
    pw = ng * cg
    assert seq % tm == 0 and max(POOL_WINDOWS) <= POOL_HALO
    return pl.pallas_call(
        functools.partial(_norm_pool_kernel, tiles_per_seq=seq // tm),
        grid=(m // tm,),
        in_specs=[pl.BlockSpec((tm, d), lambda i: (i, 0)),
                  pl.BlockSpec((1, d), lambda i: (0, 0)),
                  pl.BlockSpec((d, pw), lambda i: (0, 0), pipeline_mode=pl.Buffered(1)),
                  pl.BlockSpec((ng, cg, cg), lambda i: (0, 0, 0), pipeline_mode=pl.Buffered(1)),
                  pl.BlockSpec((1, pw), lambda i: (0, 0))],
        out_specs=[pl.BlockSpec((tm, d), lambda i: (i, 0)),
                   pl.BlockSpec((tm, pw), lambda i: (i, 0))],
        out_shape=[jax.ShapeDtypeStruct((m, d), jnp.bfloat16),
                   jax.ShapeDtypeStruct((m, pw), jnp.bfloat16)],
        scratch_shapes=[pltpu.VMEM((d, pw), jnp.bfloat16),
                        pltpu.VMEM((ng, cg, cg), jnp.bfloat16),
                        pltpu.VMEM((POOL_HALO, pw), jnp.float32)],
        compiler_params=_params(("arbitrary",)),
        name="norm_pool",
    )(x2d, g.reshape(1, d), w, w_pool, pool_scale.reshape(1, pw))


def _rope(t, cos_t, sin_t):
    width = t.shape[-1]
    half = HEAD_DIM // 2
    lane = lax.broadcasted_iota(jnp.int32, t.shape, 1)
    first_half = (lane % HEAD_DIM) < half
    partner = jnp.where(first_half,
                        pltpu.roll(t, width - half, axis=1),
                        pltpu.roll(t, half, axis=1))
    return t * cos_t + partner * sin_t


def _in_proj_kernel(a_ref, w_ref, src_ref, o_ref, dst_ref, wbf_ref):
    @pl.when(pl.program_id(1) == 0)
    def _():
        wbf_ref[...] = w_ref[...].astype(jnp.bfloat16)

    dst_ref[...] = src_ref[...].astype(jnp.bfloat16)

    o_ref[...] = jnp.dot(a_ref[...], wbf_ref[...],
                         preferred_element_type=jnp.float32).astype(o_ref.dtype)


def _in_proj_rope_kernel(a_ref, w_ref, cos_ref, sin_ref, o_ref, wbf_ref, *, n_q_tiles):
    @pl.when(pl.program_id(1) == 0)
    def _():
        wbf_ref[...] = w_ref[...].astype(jnp.bfloat16)

    y = jnp.dot(a_ref[...], wbf_ref[...], preferred_element_type=jnp.float32)
    scale = jnp.where(pl.program_id(0) < n_q_tiles, LOG2_E / math.sqrt(HEAD_DIM), 1.0)
    cos_t = cos_ref[...] * scale
    sin_t = sin_ref[...] * scale
    for h in range(y.shape[1] // HEAD_WIDTH):
        cols = slice(h * HEAD_WIDTH, (h + 1) * HEAD_WIDTH)
        o_ref[:, cols] = _rope(y[:, cols], cos_t, sin_t).astype(o_ref.dtype)


def _in_proj_rope(a, w, cos_t, sin_t, *, col0, ncols, q_cols, tm=1024, tn=1024):
    m, k = a.shape
    seq = cos_t.shape[0]
    assert col0 % tn == 0 and q_cols % tn == 0 and seq % tm == 0
    jb = col0 // tn
    sb = seq // tm
    return pl.pallas_call(
        functools.partial(_in_proj_rope_kernel, n_q_tiles=q_cols // tn),
        grid=(ncols // tn, m // tm),
        in_specs=[pl.BlockSpec((tm, k), lambda j, i: (i, 0)),
                  pl.BlockSpec((k, tn), lambda j, i: (0, j + jb)),
                  pl.BlockSpec((tm, HEAD_WIDTH), lambda j, i: (i % sb, 0)),
                  pl.BlockSpec((tm, HEAD_WIDTH), lambda j, i: (i % sb, 0))],
        out_specs=pl.BlockSpec((tm, tn), lambda j, i: (i, j)),
        out_shape=jax.ShapeDtypeStruct((m, ncols), jnp.bfloat16),
        scratch_shapes=[pltpu.VMEM((k, tn), jnp.bfloat16)],
        compiler_params=_params(("arbitrary", "arbitrary")),
        name="in_proj_qk",
    )(a, w, cos_t, sin_t)


def _in_proj_convert(a, w, src, *, col0, ncols, tm=1024, tn=1024):
    m, k = a.shape
    rows, d = src.shape
    assert col0 % tn == 0
    jb = col0 // tn
    nb, mb = ncols // tn, m // tm
    slab, rem = divmod(rows, nb * mb)
    assert rem == 0 and slab % BF16_SUBLANES == 0
    return pl.pallas_call(
        _in_proj_kernel,
        grid=(nb, mb),
        in_specs=[pl.BlockSpec((tm, k), lambda j, i: (i, 0)),
                  pl.BlockSpec((k, tn), lambda j, i: (0, j + jb)),
                  pl.BlockSpec((slab, d), lambda j, i: (j * mb + i, 0))],
        out_specs=[pl.BlockSpec((tm, tn), lambda j, i: (i, j)),
                   pl.BlockSpec((slab, d), lambda j, i: (j * mb + i, 0))],
        out_shape=[jax.ShapeDtypeStruct((m, ncols), jnp.bfloat16),
                   jax.ShapeDtypeStruct((rows, d), jnp.bfloat16)],
        scratch_shapes=[pltpu.VMEM((k, tn), jnp.bfloat16)],
        compiler_params=_params(("arbitrary", "arbitrary")),
        name="in_proj_v",
    )(a, w, src)


def _attn_kernel(q_ref, k_ref, v_ref, lq1_ref, lk1_ref, lq2_ref, lk2_ref, gain_ref,
                 o_ref, vt_ref, st2_ref, *, tq, kc):
    seq = q_ref.shape[1]
    n_tiles = seq // tq
    hw = HEAD_WIDTH

    vt_ref[:hw, :] = v_ref[0].astype(jnp.float32).T.astype(jnp.bfloat16)
    vt_ref[hw:, :] = jnp.ones((vt_ref.shape[0] - hw, seq), jnp.bfloat16)

    lam = (jnp.exp(jnp.sum(lq1_ref[...] * lk1_ref[...], axis=-1, keepdims=True))
           - jnp.exp(jnp.sum(lq2_ref[...] * lk2_ref[...], axis=-1, keepdims=True))
           + LAMBDA_INIT)

    lane = lax.broadcasted_iota(jnp.int32, (tq, hw), 1)
    key = lax.broadcasted_iota(jnp.int32, (kc, 2 * tq), 0)
    qry = lax.broadcasted_iota(jnp.int32, (kc, 2 * tq), 1) % tq

    def n_chunks(i):
        return (i + 1) * tq // kc

    def masked_queries(i):
        q = q_ref[0, i * tq:(i + 1) * tq, :]
        return jnp.concatenate([jnp.where(lane < HEAD_DIM, q, 0),
                                jnp.where(lane >= HEAD_DIM, q, 0)], axis=0)

    def score_chunk(i, c, q2):
        rows = slice(c * kc, (c + 1) * kc)
        s = lax.dot_general(k_ref[0, rows, :], q2, (((1,), (1,)), ((), ())),
                            preferred_element_type=jnp.float32)
        key0 = c * kc - i * tq
        if key0 >= 0:
            s = jnp.where((qry // CHUNK) >= ((key + key0) // CHUNK), s, _NEG_BIG)
        st2_ref[i % 2, rows, :] = s
        return jnp.max(s, axis=0, keepdims=True)

    def value_chunk(i, c, m):
        rows = slice(c * kc, (c + 1) * kc)
        p = jnp.exp2(st2_ref[i % 2, rows, :] - m).astype(jnp.bfloat16)
        return jnp.dot(vt_ref[:, rows], p, preferred_element_type=jnp.float32)

    def running_max(m, mc):
        return mc if m is None else jnp.maximum(m, mc)

    q2_next = masked_queries(0)
    m_next = None
    for c in range(n_chunks(0)):
        m_next = running_max(m_next, score_chunk(0, c, q2_next))
    for i in range(n_tiles):
        m_cur, m_next = m_next, None
        has_next = i + 1 < n_tiles
        if has_next:
            q2_next = masked_queries(i + 1)
        acc = None
        for c in range(n_chunks(i + 1)):
            if has_next:
                m_next = running_max(m_next, score_chunk(i + 1, c, q2_next))
            if c < n_chunks(i):
                part = value_chunk(i, c, m_cur)
                acc = part if acc is None else acc + part
        ot = acc[:hw] * (1.0 / acc[hw:hw + 1])
        o = (ot[:, :tq] - lam * ot[:, tq:]).T
        o = _rms_scale(o) * gain_ref[...] * (1.0 - LAMBDA_INIT)
        o_ref[0, i * tq:(i + 1) * tq, :] = o.astype(o_ref.dtype)


def _attention(qk3, v3, lq1, lk1, lq2, lk2, gain, n_heads, tq=256, kc=256):
    b, s, _ = v3.shape
    hw = HEAD_WIDTH
    vec = lambda a: a.reshape(1, -1)
    small = lambda n: pl.BlockSpec((1, n), lambda bi, h: (0, 0))
    return pl.pallas_call(
        functools.partial(_attn_kernel, tq=tq, kc=kc),
        grid=(b, n_heads),
        in_specs=[pl.BlockSpec((1, s, hw), lambda bi, h: (bi, 0, h)),
                  pl.BlockSpec((1, s, hw), lambda bi, h: (bi, 0, n_heads + h)),
                  pl.BlockSpec((1, s, hw), lambda bi, h: (bi, 0, h)),
                  small(HEAD_DIM), small(HEAD_DIM), small(HEAD_DIM), small(HEAD_DIM),
                  small(hw)],
        out_specs=pl.BlockSpec((1, s, hw), lambda bi, h: (bi, 0, h)),
        out_shape=jax.ShapeDtypeStruct((b, s, n_heads * hw), jnp.bfloat16),
        scratch_shapes=[pltpu.VMEM((hw + BF16_SUBLANES, s), jnp.bfloat16),
                        pltpu.VMEM((2, s, 2 * tq), jnp.float32)],
        compiler_params=_params(("arbitrary", "arbitrary")),
        name="diff_attn",
    )(qk3, qk3, v3, vec(lq1), vec(lk1), vec(lq2), vec(lk2), vec(gain))


def _out_proj_kernel(p_ref, a_ref, x_ref, w_ref, g_ref, x1_ref, h_ref):
    kp = p_ref.shape[1]
    for rows in _row_subtiles(x_ref.shape[0]):
        y = jnp.dot(p_ref[rows, :], w_ref[:kp, :], preferred_element_type=jnp.float32)
        y += jnp.dot(a_ref[rows, :], w_ref[kp:, :], preferred_element_type=jnp.float32)
        x1_ref[rows, :] = x_ref[rows, :] + y
        h_ref[rows, :] = (_rms_scale(x1_ref[rows, :]) * g_ref[...]).astype(h_ref.dtype)


def _out_proj(pool_out, attn_out, x2d, w_out_bf16, g, tm=512):
    m, d = x2d.shape
    kp, ka = pool_out.shape[1], attn_out.shape[1]
    return pl.pallas_call(
        _out_proj_kernel,
        grid=(m // tm,),
        in_specs=[pl.BlockSpec((tm, kp), lambda i: (i, 0)),
                  pl.BlockSpec((tm, ka), lambda i: (i, 0)),
                  pl.BlockSpec((tm, d), lambda i: (i, 0)),
                  pl.BlockSpec((kp + ka, d), lambda i: (0, 0),
                               pipeline_mode=pl.Buffered(1)),
                  pl.BlockSpec((1, d), lambda i: (0, 0))],
        out_specs=[pl.BlockSpec((tm, d), lambda i: (i, 0)),
                   pl.BlockSpec((tm, d), lambda i: (i, 0))],
        out_shape=[jax.ShapeDtypeStruct((m, d), jnp.float32),
                   jax.ShapeDtypeStruct((m, d), jnp.bfloat16)],
        compiler_params=_params(("arbitrary",)),
        name="out_proj",
    )(pool_out, attn_out, x2d, w_out_bf16, g.reshape(1, d))


def _gate_up_kernel(a_ref, wg_ref, wu_ref, wd_ref, o_ref, wdb_ref, wgb_ref, wub_ref):
    @pl.when(pl.program_id(1) == 0)
    def _():
        wgb_ref[...] = wg_ref[...].astype(jnp.bfloat16)
        wub_ref[...] = wu_ref[...].astype(jnp.bfloat16)

    wdb_ref[...] = wd_ref[...].astype(jnp.bfloat16)

    a = a_ref[...]
    gate = jnp.dot(a, wgb_ref[...], preferred_element_type=jnp.float32)
    up = jnp.dot(a, wub_ref[...], preferred_element_type=jnp.float32)
    o_ref[...] = (gate * jax.nn.sigmoid(gate) * up).astype(o_ref.dtype)


def _gate_up(a, w_gate_up, w_down, tm=1024, th=512):
    m, k = a.shape
    hidden, d = w_down.shape
    nb = hidden // th
    mb = m // tm
    slab, rem = divmod(hidden, nb * mb)
    assert rem == 0 and slab % 16 == 0, "w_down must split into bf16-tileable slabs per step"
    return pl.pallas_call(
        _gate_up_kernel,
        grid=(nb, mb),
        in_specs=[pl.BlockSpec((tm, k), lambda j, i: (i, 0)),
                  pl.BlockSpec((k, th), lambda j, i: (0, j)),
                  pl.BlockSpec((k, th), lambda j, i: (0, j + nb)),
                  pl.BlockSpec((slab, d), lambda j, i: (j * mb + i, 0))],
        out_specs=[pl.BlockSpec((tm, th), lambda j, i: (i, j)),
                   pl.BlockSpec((slab, d), lambda j, i: (j * mb + i, 0))],
        out_shape=[jax.ShapeDtypeStruct((m, hidden), jnp.bfloat16),
                   jax.ShapeDtypeStruct((hidden, d), jnp.bfloat16)],
        scratch_shapes=[pltpu.VMEM((k, th), jnp.bfloat16),
                        pltpu.VMEM((k, th), jnp.bfloat16)],
        compiler_params=_params(("arbitrary", "arbitrary")),
        name="gate_up",
    )(a, w_gate_up, w_gate_up, w_down)


def _down_kernel(a_ref, w_ref, x1_ref, g_ref, o_ref):
    for rows in _row_subtiles(o_ref.shape[0]):
        o_ref[rows, :] = x1_ref[rows, :] + jnp.dot(a_ref[rows, :], w_ref[...],
                                                   preferred_element_type=jnp.float32)
        o_ref[rows, :] = _rms_scale(o_ref[rows, :]) * g_ref[...]


def _down(a, w_down_bf16, x1, g, tm=512):
    m, hidden = a.shape
    d = w_down_bf16.shape[1]
    return pl.pallas_call(
        _down_kernel,
        grid=(m // tm,),
        in_specs=[pl.BlockSpec((tm, hidden), lambda i: (i, 0)),
                  pl.BlockSpec((hidden, d), lambda i: (0, 0), pipeline_mode=pl.Buffered(1)),
                  pl.BlockSpec((tm, d), lambda i: (i, 0)),
                  pl.BlockSpec((1, d), lambda i: (0, 0))],
        out_specs=pl.BlockSpec((tm, d), lambda i: (i, 0)),
        out_shape=jax.ShapeDtypeStruct((m, d), jnp.float32),
        compiler_params=_params(("arbitrary",)),
        name="down",
    )(a, w_down_bf16, x1, g.reshape(1, d))


def _rotary_tables(seq):
    inv_freq = ROPE_THETA ** (-jnp.arange(0, HEAD_DIM, 2, dtype=jnp.float32) / HEAD_DIM)
    ang = jnp.arange(seq, dtype=jnp.float32)[:, None] * inv_freq[None, :]
    cos, sin = jnp.cos(ang), jnp.sin(ang)
    reps = HEAD_WIDTH // HEAD_DIM
    cos_t = jnp.tile(jnp.concatenate([cos, cos], axis=-1), (1, reps))
    sin_t = jnp.tile(jnp.concatenate([-sin, sin], axis=-1), (1, reps))
    return cos_t, sin_t


def kernel(x, norm_mix, w_in, w_pool, pool_scale, lambda_q1, lambda_k1, lambda_q2,
           lambda_k2, subln_gain, w_out, norm_ffn, w_gate_up, w_down, norm_final):
    b, s, d = x.shape
    depth = w_in.shape[0]
    assert depth == 1, "single-layer block"
    pool_width = w_pool.shape[1] * w_pool.shape[2]
    attn_width = w_out.shape[1] - pool_width
    n_heads = attn_width // HEAD_WIDTH

    cos_t, sin_t = _rotary_tables(s)
    x2d = x.reshape(b * s, d)
    l = 0
    h1, pool_out = _norm_pool(x2d, norm_mix[l], w_in[l], w_pool[l], pool_scale[l], s)
    qk_width = n_heads * HEAD_WIDTH
    qk = _in_proj_rope(h1, w_in[l], cos_t, sin_t, col0=pool_width, ncols=2 * qk_width,
                       q_cols=qk_width)
    v, w_out_bf16 = _in_proj_convert(h1, w_in[l], w_out[l], col0=pool_width + 2 * qk_width,
                                     ncols=attn_width)
    attn_out = _attention(qk.reshape(b, s, -1), v.reshape(b, s, -1), lambda_q1[l],
                          lambda_k1[l], lambda_q2[l], lambda_k2[l], subln_gain[l], n_heads)
    x1, h2 = _out_proj(pool_out, attn_out.reshape(b * s, -1), x2d,
                       w_out_bf16, norm_ffn[l])
    a, w_down_bf16 = _gate_up(h2, w_gate_up[l], w_down[l])
    out = _down(a, w_down_bf16, x1, norm_final)
    return out.reshape(b, s, d)
```

```python
import functools
import math

import jax
import jax.numpy as jnp
from jax import lax
from jax.experimental import pallas as pl
from jax.experimental.pallas import tpu as pltpu

CHUNK = 64
POOL_WINDOWS = (2, 4, 8, 16)
HEAD_DIM = 64
HEAD_WIDTH = 2 * HEAD_DIM
ROPE_THETA = 10000.0
NORM_EPS = 1e-6
LAMBDA_INIT = 0.8 - 0.6 * math.exp(-0.3 * 0)

V7X_VMEM_LIMIT_BYTES = 56 * 1024 * 1024

F32_SUBLANES = 8
BF16_SUBLANES = 16

_NEG_BIG = -1e30
LOG2_E = math.log2(math.e)


def _params(semantics):
    return pltpu.CompilerParams(dimension_semantics=semantics,
                                vmem_limit_bytes=V7X_VMEM_LIMIT_BYTES)


def _row_subtiles(rows, n=2):
    step, rem = divmod(rows, n)
    assert rem == 0
    return [slice(r * step, (r + 1) * step) for r in range(n)]


def _rms_scale(x):
    return x * lax.rsqrt(jnp.mean(x * x, axis=-1, keepdims=True) + NORM_EPS)


POOL_HALO = 16


def _shift_rows_down(s, shift):
    if shift == F32_SUBLANES:
        return jnp.concatenate([s[:shift], s[:-shift]], axis=0)
    return pltpu.roll(s, shift, axis=0)


def _pool_group(prev, u, window, is_seq_start):
    s = jnp.concatenate([prev, u], axis=0)
    shift = 1
    while shift < window:
        s = s + _shift_rows_down(s, shift)
        shift *= 2
    s = s[POOL_HALO:]
    pooled = s * (1.0 / window) - u
    cnt = (lax.broadcasted_iota(jnp.int32, (POOL_HALO, u.shape[1]), 0) + 1).astype(jnp.float32)
    cnt = jnp.where(is_seq_start, jnp.minimum(cnt, float(window)), float(window))
    top = s[:POOL_HALO] / cnt - u[:POOL_HALO]
    return jnp.concatenate([top, pooled[POOL_HALO:]], axis=0)


def _norm_pool_kernel(x_ref, g_ref, w_ref, wp_ref, ps_ref, h_ref, o_ref,
                      wbf_ref, wpb_ref, halo_ref, *, tiles_per_seq):
    i = pl.program_id(0)

    @pl.when(i == 0)
    def _():
        wbf_ref[...] = w_ref[...].astype(jnp.bfloat16)
        wpb_ref[...] = wp_ref[...].astype(jnp.bfloat16)
        halo_ref[...] = jnp.zeros_like(halo_ref)

    h = (_rms_scale(x_ref[...]) * g_ref[...]).astype(jnp.bfloat16)
    h_ref[...] = h
    u = jnp.dot(h, wbf_ref[...], preferred_element_type=jnp.float32)

    is_seq_start = (i % tiles_per_seq) == 0
    prev = jnp.where(is_seq_start, 0.0, halo_ref[...])
    halo_ref[...] = u[-POOL_HALO:]
    cg = wp_ref.shape[1]
    for gi, window in enumerate(POOL_WINDOWS):
        cols = slice(gi * cg, (gi + 1) * cg)
        pooled = _pool_group(prev[:, cols], u[:, cols], window, is_seq_start)
        mapped = jnp.dot(pooled.astype(jnp.bfloat16), wpb_ref[gi],
                         preferred_element_type=jnp.float32)
        o_ref[:, cols] = (mapped * ps_ref[:, cols]).astype(o_ref.dtype)


def _norm_pool(x2d, g, w, w_pool, pool_scale, seq, tm=512):
    m, d = x2d.shape
    ng, cg, _ = w_pool.shape
    pw = ng * cg
    assert seq % tm == 0 and max(POOL_WINDOWS) <= POOL_HALO
    return pl.pallas_call(
        functools.partial(_norm_pool_kernel, tiles_per_seq=seq // tm),
        grid=(m // tm,),
        in_specs=[pl.BlockSpec((tm, d), lambda i: (i, 0)),
                  pl.BlockSpec((1, d), lambda i: (0, 0)),
                  pl.BlockSpec((d, pw), lambda i: (0, 0), pipeline_mode=pl.Buffered(1)),
                  pl.BlockSpec((ng, cg, cg), lambda i: (0, 0, 0), pipeline_mode=pl.Buffered(1)),
                  pl.BlockSpec((1, pw), lambda i: (0, 0))],
        out_specs=[pl.BlockSpec((tm, d), lambda i: (i, 0)),
                   pl.BlockSpec((tm, pw), lambda i: (i, 0))],
        out_shape=[jax.ShapeDtypeStruct((m, d), jnp.bfloat16),
                   jax.ShapeDtypeStruct((m, pw), jnp.bfloat16)],
        scratch_shapes=[pltpu.VMEM((d, pw), jnp.bfloat16),
                        pltpu.VMEM((ng, cg, cg), jnp.bfloat16),
                        pltpu.VMEM((POOL_HALO, pw), jnp.float32)],
        compiler_params=_params(("arbitrary",)),
        name="norm_pool",
    )(x2d, g.reshape(1, d), w, w_pool, pool_scale.reshape(1, pw))


def _rope(t, cos_t, sin_t):
    width = t.shape[-1]
    half = HEAD_DIM // 2
    lane = lax.broadcasted_iota(jnp.int32, t.shape, 1)
    first_half = (lane % HEAD_DIM) < half
    partner = jnp.where(first_half,
                        pltpu.roll(t, width - half, axis=1),
                        pltpu.roll(t, half, axis=1))
    return t * cos_t + partner * sin_t


def _in_proj_kernel(a_ref, w_ref, src_ref, o_ref, dst_ref, wbf_ref):
    @pl.when(pl.program_id(1) == 0)
    def _():
        wbf_ref[...] = w_ref[...].astype(jnp.bfloat16)

    dst_ref[...] = src_ref[...].astype(jnp.bfloat16)

    o_ref[...] = jnp.dot(a_ref[...], wbf_ref[...],
                         preferred_element_type=jnp.float32).astype(o_ref.dtype)


def _in_proj_rope_kernel(a_ref, w_ref, cos_ref, sin_ref, o_ref, wbf_ref, *, n_q_tiles):
    @pl.when(pl.program_id(1) == 0)
    def _():
        wbf_ref[...] = w_ref[...].astype(jnp.bfloat16)

    scale = jnp.where(pl.program_id(0) < n_q_tiles, LOG2_E / math.sqrt(HEAD_DIM), 1.0)
    for rows in _row_subtiles(a_ref.shape[0], 4):
        y = jnp.dot(a_ref[rows, :], wbf_ref[...], preferred_element_type=jnp.float32)
        cos_t = cos_ref[rows, :] * scale
        sin_t = sin_ref[rows, :] * scale
        for h in range(y.shape[1] // HEAD_WIDTH):
            cols = slice(h * HEAD_WIDTH, (h + 1) * HEAD_WIDTH)
            o_ref[rows, cols] = _rope(y[:, cols], cos_t, sin_t).astype(o_ref.dtype)


def _in_proj_rope(a, w, cos_t, sin_t, *, col0, ncols, q_cols, tm=1024, tn=1024):
    m, k = a.shape
    seq = cos_t.shape[0]
    assert col0 % tn == 0 and q_cols % tn == 0 and seq % tm == 0
    jb = col0 // tn
    sb = seq // tm
    return pl.pallas_call(
        functools.partial(_in_proj_rope_kernel, n_q_tiles=q_cols // tn),
        grid=(ncols // tn, m // tm),
        in_specs=[pl.BlockSpec((tm, k), lambda j, i: (i, 0)),
                  pl.BlockSpec((k, tn), lambda j, i: (0, j + jb)),
                  pl.BlockSpec((tm, HEAD_WIDTH), lambda j, i: (i % sb, 0)),
                  pl.BlockSpec((tm, HEAD_WIDTH), lambda j, i: (i % sb, 0))],
        out_specs=pl.BlockSpec((tm, tn), lambda j, i: (i, j)),
        out_shape=jax.ShapeDtypeStruct((m, ncols), jnp.bfloat16),
        scratch_shapes=[pltpu.VMEM((k, tn), jnp.bfloat16)],
        compiler_params=_params(("arbitrary", "arbitrary")),
        name="in_proj_qk",
    )(a, w, cos_t, sin_t)


def _in_proj_convert(a, w, src, *, col0, ncols, tm=1024, tn=1024):
    m, k = a.shape
    rows, d = src.shape
    assert col0 % tn == 0
    jb = col0 // tn
    nb, mb = ncols // tn, m // tm
    slab, rem = divmod(rows, nb * mb)
    assert rem == 0 and slab % BF16_SUBLANES == 0
    return pl.pallas_call(
        _in_proj_kernel,
        grid=(nb, mb),
        in_specs=[pl.BlockSpec((tm, k), lambda j, i: (i, 0)),
                  pl.BlockSpec((k, tn), lambda j, i: (0, j + jb)),
                  pl.BlockSpec((slab, d), lambda j, i: (j * mb + i, 0))],
        out_specs=[pl.BlockSpec((tm, tn), lambda j, i: (i, j)),
                   pl.BlockSpec((slab, d), lambda j, i: (j * mb + i, 0))],
        out_shape=[jax.ShapeDtypeStruct((m, ncols), jnp.bfloat16),
                   jax.ShapeDtypeStruct((rows, d), jnp.bfloat16)],
        scratch_shapes=[pltpu.VMEM((k, tn), jnp.bfloat16)],
        compiler_params=_params(("arbitrary", "arbitrary")),
        name="in_proj_v",
    )(a, w, src)


def _attn_kernel(q_ref, k_ref, v_ref, lq1_ref, lk1_ref, lq2_ref, lk2_ref, gain_ref,
                 o_ref, vt_ref, st2_ref, *, tq, kc):
    seq = q_ref.shape[1]
    n_tiles = seq // tq
    hw = HEAD_WIDTH

    vt_ref[:hw, :] = v_ref[0].astype(jnp.float32).T.astype(jnp.bfloat16)
    vt_ref[hw:, :] = jnp.ones((vt_ref.shape[0] - hw, seq), jnp.bfloat16)

    lam = (jnp.exp(jnp.sum(lq1_ref[...] * lk1_ref[...], axis=-1, keepdims=True))
           - jnp.exp(jnp.sum(lq2_ref[...] * lk2_ref[...], axis=-1, keepdims=True))
           + LAMBDA_INIT)

    dim = lax.broadcasted_iota(jnp.int32, (hw, tq), 0)
    key = lax.broadcasted_iota(jnp.int32, (kc, 2 * tq), 0)
    qry = lax.broadcasted_iota(jnp.int32, (kc, 2 * tq), 1) % tq

    def n_chunks(i):
        return (i + 1) * tq // kc

    def masked_queries(i):
        qt = q_ref[0, i * tq:(i + 1) * tq, :].astype(jnp.float32).T
        return jnp.concatenate([jnp.where(dim < HEAD_DIM, qt, 0.0),
                                jnp.where(dim >= HEAD_DIM, qt, 0.0)],
                               axis=1).astype(jnp.bfloat16)

    def score_chunk(i, c, q2t):
        rows = slice(c * kc, (c + 1) * kc)
        s = jnp.dot(k_ref[0, rows, :], q2t, preferred_element_type=jnp.float32)
        key0 = c * kc - i * tq
        if key0 >= 0:
            s = jnp.where((qry // CHUNK) >= ((key + key0) // CHUNK), s, _NEG_BIG)
        st2_ref[i % 2, rows, :] = s
        return jnp.max(s, axis=0, keepdims=True)

    def value_chunk(i, c, m):
        rows = slice(c * kc, (c + 1) * kc)
        p = jnp.exp2(st2_ref[i % 2, rows, :] - m).astype(jnp.bfloat16)
        return jnp.dot(vt_ref[:, rows], p, preferred_element_type=jnp.float32)

    def running_max(m, mc):
        return mc if m is None else jnp.maximum(m, mc)

    q2_next = masked_queries(0)
    m_next = None
    for c in range(n_chunks(0)):
        m_next = running_max(m_next, score_chunk(0, c, q2_next))
    for i in range(n_tiles):
        m_cur, m_next = m_next, None
        has_next = i + 1 < n_tiles
        if has_next:
            q2_next = masked_queries(i + 1)
        acc = None
        for c in range(n_chunks(i + 1)):
            if has_next:
                m_next = running_max(m_next, score_chunk(i + 1, c, q2_next))
            if c < n_chunks(i):
                part = value_chunk(i, c, m_cur)
                acc = part if acc is None else acc + part
        ot = acc[:hw] * (1.0 / acc[hw:hw + 1])
        o = (ot[:, :tq] - lam * ot[:, tq:]).T
        o = _rms_scale(o) * gain_ref[...] * (1.0 - LAMBDA_INIT)
        o_ref[0, i * tq:(i + 1) * tq, :] = o.astype(o_ref.dtype)


def _attention(qk3, v3, lq1, lk1, lq2, lk2, gain, n_heads, tq=256, kc=256):
    b, s, _ = v3.shape
    hw = HEAD_WIDTH
    vec = lambda a: a.reshape(1, -1)
    small = lambda n: pl.BlockSpec((1, n), lambda bi, h: (0, 0))
    return pl.pallas_call(
        functools.partial(_attn_kernel, tq=tq, kc=kc),
        grid=(b, n_heads),
        in_specs=[pl.BlockSpec((1, s, hw), lambda bi, h: (bi, 0, h)),
                  pl.BlockSpec((1, s, hw), lambda bi, h: (bi, 0, n_heads + h)),
                  pl.BlockSpec((1, s, hw), lambda bi, h: (bi, 0, h)),
                  small(HEAD_DIM), small(HEAD_DIM), small(HEAD_DIM), small(HEAD_DIM),
                  small(hw)],
        out_specs=pl.BlockSpec((1, s, hw), lambda bi, h: (bi, 0, h)),
        out_shape=jax.ShapeDtypeStruct((b, s, n_heads * hw), jnp.bfloat16),
        scratch_shapes=[pltpu.VMEM((hw + BF16_SUBLANES, s), jnp.bfloat16),
                        pltpu.VMEM((2, s, 2 * tq), jnp.float32)],
        compiler_params=_params(("arbitrary", "arbitrary")),
        name="diff_attn",
    )(qk3, qk3, v3, vec(lq1), vec(lk1), vec(lq2), vec(lk2), vec(gain))


def _out_proj_kernel(p_ref, a_ref, x_ref, w_ref, g_ref, x1_ref, h_ref):
    kp = p_ref.shape[1]
    for rows in _row_subtiles(x_ref.shape[0]):
        y = jnp.dot(p_ref[rows, :], w_ref[:kp, :], preferred_element_type=jnp.float32)
        y += jnp.dot(a_ref[rows, :], w_ref[kp:, :], preferred_element_type=jnp.float32)
        x1_ref[rows, :] = x_ref[rows, :] + y
        h_ref[rows, :] = (_rms_scale(x1_ref[rows, :]) * g_ref[...]).astype(h_ref.dtype)


def _out_proj(pool_out, attn_out, x2d, w_out_bf16, g, tm=512):
    m, d = x2d.shape
    kp, ka = pool_out.shape[1], attn_out.shape[1]
    return pl.pallas_call(
        _out_proj_kernel,
        grid=(m // tm,),
        in_specs=[pl.BlockSpec((tm, kp), lambda i: (i, 0)),
                  pl.BlockSpec((tm, ka), lambda i: (i, 0)),
                  pl.BlockSpec((tm, d), lambda i: (i, 0)),
                  pl.BlockSpec((kp + ka, d), lambda i: (0, 0),
                               pipeline_mode=pl.Buffered(1)),
                  pl.BlockSpec((1, d), lambda i: (0, 0))],
        out_specs=[pl.BlockSpec((tm, d), lambda i: (i, 0)),
                   pl.BlockSpec((tm, d), lambda i: (i, 0))],
        out_shape=[jax.ShapeDtypeStruct((m, d), jnp.float32),
                   jax.ShapeDtypeStruct((m, d), jnp.bfloat16)],
        compiler_params=_params(("arbitrary",)),
        name="out_proj",
    )(pool_out, attn_out, x2d, w_out_bf16, g.reshape(1, d))


def _gate_up_kernel(a_ref, wg_ref, wu_ref, wd_ref, o_ref, wdb_ref, wgb_ref, wub_ref):
    @pl.when(pl.program_id(1) == 0)
    def _():
        wgb_ref[...] = wg_ref[...].astype(jnp.bfloat16)
        wub_ref[...] = wu_ref[...].astype(jnp.bfloat16)

    wdb_ref[...] = wd_ref[...].astype(jnp.bfloat16)

    a = a_ref[...]
    gate = jnp.dot(a, wgb_ref[...], preferred_element_type=jnp.float32)
    up = jnp.dot(a, wub_ref[...], preferred_element_type=jnp.float32)
    o_ref[...] = (gate * jax.nn.sigmoid(gate) * up).astype(o_ref.dtype)


def _gate_up(a, w_gate_up, w_down, tm=1024, th=512):
    m, k = a.shape
    hidden, d = w_down.shape
    nb = hidden // th
    mb = m // tm
    slab, rem = divmod(hidden, nb * mb)
    assert rem == 0 and slab % 16 == 0, "w_down must split into bf16-tileable slabs per step"
    return pl.pallas_call(
        _gate_up_kernel,
        grid=(nb, mb),
        in_specs=[pl.BlockSpec((tm, k), lambda j, i: (i, 0)),
                  pl.BlockSpec((k, th), lambda j, i: (0, j)),
                  pl.BlockSpec((k, th), lambda j, i: (0, j + nb)),
                  pl.BlockSpec((slab, d), lambda j, i: (j * mb + i, 0))],
        out_specs=[pl.BlockSpec((tm, th), lambda j, i: (i, j)),
                   pl.BlockSpec((slab, d), lambda j, i: (j * mb + i, 0))],
        out_shape=[jax.ShapeDtypeStruct((m, hidden), jnp.bfloat16),
                   jax.ShapeDtypeStruct((hidden, d), jnp.bfloat16)],
        scratch_shapes=[pltpu.VMEM((k, th), jnp.bfloat16),
                        pltpu.VMEM((k, th), jnp.bfloat16)],
        compiler_params=_params(("arbitrary", "arbitrary")),
        name="gate_up",
    )(a, w_gate_up, w_gate_up, w_down)


def _down_kernel(a_ref, w_ref, x1_ref, g_ref, o_ref):
    for rows in _row_subtiles(o_ref.shape[0]):
        o_ref[rows, :] = x1_ref[rows, :] + jnp.dot(a_ref[rows, :], w_ref[...],
                                                   preferred_element_type=jnp.float32)
        o_ref[rows, :] = _rms_scale(o_ref[rows, :]) * g_ref[...]


def _down(a, w_down_bf16, x1, g, tm=512):
    m, hidden = a.shape
    d = w_down_bf16.shape[1]
    return pl.pallas_call(
        _down_kernel,
        grid=(m // tm,),
        in_specs=[pl.BlockSpec((tm, hidden), lambda i: (i, 0)),
                  pl.BlockSpec((hidden, d), lambda i: (0, 0), pipeline_mode=pl.Buffered(1)),
                  pl.BlockSpec((tm, d), lambda i: (i, 0)),
                  pl.BlockSpec((1, d), lambda i: (0, 0))],
        out_specs=pl.BlockSpec((tm, d), lambda i: (i, 0)),
        out_shape=jax.ShapeDtypeStruct((m, d), jnp.float32),
        compiler_params=_params(("arbitrary",)),
        name="down",
    )(a, w_down_bf16, x1, g.reshape(1, d))


def _rotary_tables(seq):
    inv_freq = ROPE_THETA ** (-jnp.arange(0, HEAD_DIM, 2, dtype=jnp.float32) / HEAD_DIM)
    ang = jnp.arange(seq, dtype=jnp.float32)[:, None] * inv_freq[None, :]
    cos, sin = jnp.cos(ang), jnp.sin(ang)
    reps = HEAD_WIDTH // HEAD_DIM
    cos_t = jnp.tile(jnp.concatenate([cos, cos], axis=-1), (1, reps))
    sin_t = jnp.tile(jnp.concatenate([-sin, sin], axis=-1), (1, reps))
    return cos_t, sin_t


def kernel(x, norm_mix, w_in, w_pool, pool_scale, lambda_q1, lambda_k1, lambda_q2,
           lambda_k2, subln_gain, w_out, norm_ffn, w_gate_up, w_down, norm_final):
    b, s, d = x.shape
    depth = w_in.shape[0]
    assert depth == 1, "single-layer block"
    pool_width = w_pool.shape[1] * w_pool.shape[2]
    attn_width = w_out.shape[1] - pool_width
    n_heads = attn_width // HEAD_WIDTH

    cos_t, sin_t = _rotary_tables(s)
    x2d = x.reshape(b * s, d)
    l = 0
    h1, pool_out = _norm_pool(x2d, norm_mix[l], w_in[l], w_pool[l], pool_scale[l], s)
    qk_width = n_heads * HEAD_WIDTH
    qk = _in_proj_rope(h1, w_in[l], cos_t, sin_t, col0=pool_width, ncols=2 * qk_width,
                       q_cols=qk_width)
    v, w_out_bf16 = _in_proj_convert(h1, w_in[l], w_out[l], col0=pool_width + 2 * qk_width,
                                     ncols=attn_width)
    attn_out = _attention(qk.reshape(b, s, -1), v.reshape(b, s, -1), lambda_q1[l],
                          lambda_k1[l], lambda_q2[l], lambda_k2[l], subln_gain[l], n_heads)
    x1, h2 = _out_proj(pool_out, attn_out.reshape(b * s, -1), x2d,
                       w_out_bf16, norm_ffn[l])
    a, w_down_bf16 = _gate_up(h2, w_gate_up[l], w_down[l])
    out = _down(a, w_down_bf16, x1, norm_final)
    return out.reshape(b, s, d)
```

```python
import functools
import math

import jax
import jax.numpy as jnp
from jax import lax
from jax.experimental import pallas as pl
from jax.experimental.pallas import tpu as pltpu

CHUNK = 64
POOL_WINDOWS = (2, 4, 8, 16)
HEAD_DIM = 64
HEAD_WIDTH = 2 * HEAD_DIM
ROPE_THETA = 10000.0
NORM_EPS = 1e-6
LAMBDA_INIT = 0.8 - 0.6 * math.exp(-0.3 * 0)

V7X_VMEM_LIMIT_BYTES = 56 * 1024 * 1024

F32_SUBLANES = 8
BF16_SUBLANES = 16

_NEG_BIG = -1e30
LOG2_E = math.log2(math.e)


def _params(semantics):
    return pltpu.CompilerParams(dimension_semantics=semantics,
                                vmem_limit_bytes=V7X_VMEM_LIMIT_BYTES)


def _row_subtiles(rows, n=2):
    step, rem = divmod(rows, n)
    assert rem == 0
    return [slice(r * step, (r + 1) * step) for r in range(n)]


def _rms_scale(x):
    return x * lax.rsqrt(jnp.mean(x * x, axis=-1, keepdims=True) + NORM_EPS)


POOL_HALO = 16


def _shift_rows_down(s, shift):
    if shift == F32_SUBLANES:
        return jnp.concatenate([s[:shift], s[:-shift]], axis=0)
    return pltpu.roll(s, shift, axis=0)


def _pool_group(prev, u, window, is_seq_start):
    s = jnp.concatenate([prev, u], axis=0)
    shift = 1
    while shift < window:
        s = s + _shift_rows_down(s, shift)
        shift *= 2
    s = s[POOL_HALO:]
    pooled = s * (1.0 / window) - u
    cnt = (lax.broadcasted_iota(jnp.int32, (POOL_HALO, u.shape[1]), 0) + 1).astype(jnp.float32)
    cnt = jnp.where(is_seq_start, jnp.minimum(cnt, float(window)), float(window))
    top = s[:POOL_HALO] / cnt - u[:POOL_HALO]
    return jnp.concatenate([top, pooled[POOL_HALO:]], axis=0)


def _norm_pool_kernel(x_ref, g_ref, w_ref, wp_ref, ps_ref, h_ref, o_ref,
                      wbf_ref, wpb_ref, halo_ref, *, tiles_per_seq):
    i = pl.program_id(0)

    @pl.when(i == 0)
    def _():
        wbf_ref[...] = w_ref[...].astype(jnp.bfloat16)
        wpb_ref[...] = wp_ref[...].astype(jnp.bfloat16)
        halo_ref[...] = jnp.zeros_like(halo_ref)

    h = (_rms_scale(x_ref[...]) * g_ref[...]).astype(jnp.bfloat16)
    h_ref[...] = h
    u = jnp.dot(h, wbf_ref[...], preferred_element_type=jnp.float32)

    is_seq_start = (i % tiles_per_seq) == 0
    prev = jnp.where(is_seq_start, 0.0, halo_ref[...])
    halo_ref[...] = u[-POOL_HALO:]
    cg = wp_ref.shape[1]
    for gi, window in enumerate(POOL_WINDOWS):
        cols = slice(gi * cg, (gi + 1) * cg)
        pooled = _pool_group(prev[:, cols], u[:, cols], window, is_seq_start)
        mapped = jnp.dot(pooled.astype(jnp.bfloat16), wpb_ref[gi],
                         preferred_element_type=jnp.float32)
        o_ref[:, cols] = (mapped * ps_ref[:, cols]).astype(o_ref.dtype)


def _norm_pool(x2d, g, w, w_pool, pool_scale, seq, tm=512):
    m, d = x2d.shape
    ng, cg, _ = w_pool.shape
    pw = ng * cg
    assert seq % tm == 0 and max(POOL_WINDOWS) <= POOL_HALO
    return pl.pallas_call(
        functools.partial(_norm_pool_kernel, tiles_per_seq=seq // tm),
        grid=(m // tm,),
        in_specs=[pl.BlockSpec((tm, d), lambda i: (i, 0)),
                  pl.BlockSpec((1, d), lambda i: (0, 0)),
                  pl.BlockSpec((d, pw), lambda i: (0, 0), pipeline_mode=pl.Buffered(1)),
                  pl.BlockSpec((ng, cg, cg), lambda i: (0, 0, 0), pipeline_mode=pl.Buffered(1)),
                  pl.BlockSpec((1, pw), lambda i: (0, 0))],
        out_specs=[pl.BlockSpec((tm, d), lambda i: (i, 0)),
                   pl.BlockSpec((tm, pw), lambda i: (i, 0))],
        out_shape=[jax.ShapeDtypeStruct((m, d), jnp.bfloat16),
                   jax.ShapeDtypeStruct((m, pw), jnp.bfloat16)],
        scratch_shapes=[pltpu.VMEM((d, pw), jnp.bfloat16),
                        pltpu.VMEM((ng, cg, cg), jnp.bfloat16),
                        pltpu.VMEM((POOL_HALO, pw), jnp.float32)],
        compiler_params=_params(("arbitrary",)),
        name="norm_pool",
    )(x2d, g.reshape(1, d), w, w_pool, pool_scale.reshape(1, pw))


def _rope(t, cos_t, sin_t):
    width = t.shape[-1]
    half = HEAD_DIM // 2
    lane = lax.broadcasted_iota(jnp.int32, t.shape, 1)
    first_half = (lane % HEAD_DIM) < half
    partner = jnp.where(first_half,
                        pltpu.roll(t, width - half, axis=1),
                        pltpu.roll(t, half, axis=1))
    return t * cos_t + partner * sin_t


def _in_proj_kernel(a_ref, w_ref, src_ref, o_ref, dst_ref, wbf_ref):
    @pl.when(pl.program_id(1) == 0)
    def _():
        wbf_ref[...] = w_ref[...].astype(jnp.bfloat16)

    dst_ref[...] = src_ref[...].astype(jnp.bfloat16)

    o_ref[...] = jnp.dot(a_ref[...], wbf_ref[...],
                         preferred_element_type=jnp.float32).astype(o_ref.dtype)


def _in_proj_rope_kernel(a_ref, w_ref, cos_ref, sin_ref, o_ref, wbf_ref, *, n_q_tiles):
    @pl.when(pl.program_id(1) == 0)
    def _():
        wbf_ref[...] = w_ref[...].astype(jnp.bfloat16)

    scale = jnp.where(pl.program_id(0) < n_q_tiles, LOG2_E / math.sqrt(HEAD_DIM), 1.0)
    for rows in _row_subtiles(a_ref.shape[0], 4):
        y = jnp.dot(a_ref[rows, :], wbf_ref[...], preferred_element_type=jnp.float32)
        cos_t = cos_ref[rows, :] * scale
        sin_t = sin_ref[rows, :] * scale
        for h in range(y.shape[1] // HEAD_WIDTH):
            cols = slice(h * HEAD_WIDTH, (h + 1) * HEAD_WIDTH)
            o_ref[rows, cols] = _rope(y[:, cols], cos_t, sin_t).astype(o_ref.dtype)


def _in_proj_rope(a, w, cos_t, sin_t, *, col0, ncols, q_cols, tm=1024, tn=1024):
    m, k = a.shape
    seq = cos_t.shape[0]
    assert col0 % tn == 0 and q_cols % tn == 0 and seq % tm == 0
    jb = col0 // tn
    sb = seq // tm
    return pl.pallas_call(
        functools.partial(_in_proj_rope_kernel, n_q_tiles=q_cols // tn),
        grid=(ncols // tn, m // tm),
        in_specs=[pl.BlockSpec((tm, k), lambda j, i: (i, 0)),
                  pl.BlockSpec((k, tn), lambda j, i: (0, j + jb)),
                  pl.BlockSpec((tm, HEAD_WIDTH), lambda j, i: (i % sb, 0)),
                  pl.BlockSpec((tm, HEAD_WIDTH), lambda j, i: (i % sb, 0))],
        out_specs=pl.BlockSpec((tm, tn), lambda j, i: (i, j)),
        out_shape=jax.ShapeDtypeStruct((m, ncols), jnp.bfloat16),
        scratch_shapes=[pltpu.VMEM((k, tn), jnp.bfloat16)],
        compiler_params=_params(("arbitrary", "arbitrary")),
        name="in_proj_qk",
    )(a, w, cos_t, sin_t)


def _in_proj_convert(a, w, src, *, col0, ncols, tm=1024, tn=1024):
    m, k = a.shape
    rows, d = src.shape
    assert col0 % tn == 0
    jb = col0 // tn
    nb, mb = ncols // tn, m // tm
    slab, rem = divmod(rows, nb * mb)
    assert rem == 0 and slab % BF16_SUBLANES == 0
    return pl.pallas_call(
        _in_proj_kernel,
        grid=(nb, mb),
        in_specs=[pl.BlockSpec((tm, k), lambda j, i: (i, 0)),
                  pl.BlockSpec((k, tn), lambda j, i: (0, j + jb)),
                  pl.BlockSpec((slab, d), lambda j, i: (j * mb + i, 0))],
        out_specs=[pl.BlockSpec((tm, tn), lambda j, i: (i, j)),
                   pl.BlockSpec((slab, d), lambda j, i: (j * mb + i, 0))],
        out_shape=[jax.ShapeDtypeStruct((m, ncols), jnp.bfloat16),
                   jax.ShapeDtypeStruct((rows, d), jnp.bfloat16)],
        scratch_shapes=[pltpu.VMEM((k, tn), jnp.bfloat16)],
        compiler_params=_params(("arbitrary", "arbitrary")),
        name="in_proj_v",
    )(a, w, src)


def _attn_kernel(q_ref, k_ref, v_ref, lq1_ref, lk1_ref, lq2_ref, lk2_ref, gain_ref,
                 o_ref, vt_ref, st2_ref, *, tq, kc):
    seq = q_ref.shape[1]
    n_tiles = seq // tq
    hw = HEAD_WIDTH

    vt_ref[:hw, :] = v_ref[0].astype(jnp.float32).T.astype(jnp.bfloat16)
    vt_ref[hw:, :] = jnp.ones((vt_ref.shape[0] - hw, seq), jnp.bfloat16)

    lam = (jnp.exp(jnp.sum(lq1_ref[...] * lk1_ref[...], axis=-1, keepdims=True))
           - jnp.exp(jnp.sum(lq2_ref[...] * lk2_ref[...], axis=-1, keepdims=True))
           + LAMBDA_INIT)

    lane = lax.broadcasted_iota(jnp.int32, (tq, hw), 1)
    key = lax.broadcasted_iota(jnp.int32, (kc, 2 * tq), 0)
    qry = lax.broadcasted_iota(jnp.int32, (kc, 2 * tq), 1) % tq

    def n_chunks(i):
        return (i + 1) * tq // kc

    def masked_queries(i):
        q = q_ref[0, i * tq:(i + 1) * tq, :]
        return jnp.concatenate([jnp.where(lane < HEAD_DIM, q, 0),
                                jnp.where(lane >= HEAD_DIM, q, 0)], axis=0)

    def score_chunk(i, c, q2):
        rows = slice(c * kc, (c + 1) * kc)
        s = lax.dot_general(k_ref[0, rows, :], q2, (((1,), (1,)), ((), ())),
                            preferred_element_type=jnp.float32)
        key0 = c * kc - i * tq
        if key0 >= 0:
            s = jnp.where((qry // CHUNK) >= ((key + key0) // CHUNK), s, _NEG_BIG)
        st2_ref[i % 2, rows, :] = s
        return jnp.max(s, axis=0, keepdims=True)

    def value_chunk(i, c, m):
        rows = slice(c * kc, (c + 1) * kc)
        p = jnp.exp2(st2_ref[i % 2, rows, :] - m).astype(jnp.bfloat16)
        return jnp.dot(vt_ref[:, rows], p, preferred_element_type=jnp.float32)

    def running_max(m, mc):
        return mc if m is None else jnp.maximum(m, mc)

    last = n_tiles - 1
    q2_next = masked_queries(last)
    m_next = None
    for c in range(n_chunks(last)):
        m_next = running_max(m_next, score_chunk(last, c, q2_next))
    for i in range(last, -1, -1):
        m_cur, m_next = m_next, None
        has_next = i > 0
        if has_next:
            q2_next = masked_queries(i - 1)
        acc = None
        for c in range(n_chunks(i)):
            if has_next and c < n_chunks(i - 1):
                m_next = running_max(m_next, score_chunk(i - 1, c, q2_next))
            part = value_chunk(i, c, m_cur)
            acc = part if acc is None else acc + part
        ot = acc[:hw] * (1.0 / acc[hw:hw + 1])
        o = (ot[:, :tq] - lam * ot[:, tq:]).T
        o = _rms_scale(o) * gain_ref[...] * (1.0 - LAMBDA_INIT)
        o_ref[0, i * tq:(i + 1) * tq, :] = o.astype(o_ref.dtype)


def _attention(qk3, v3, lq1, lk1, lq2, lk2, gain, n_heads, tq=256, kc=256):
    b, s, _ = v3.shape
    hw = HEAD_WIDTH
    vec = lambda a: a.reshape(1, -1)
    small = lambda n: pl.BlockSpec((1, n), lambda bi, h: (0, 0))
    return pl.pallas_call(
        functools.partial(_attn_kernel, tq=tq, kc=kc),
        grid=(b, n_heads),
        in_specs=[pl.BlockSpec((1, s, hw), lambda bi, h: (bi, 0, h)),
                  pl.BlockSpec((1, s, hw), lambda bi, h: (bi, 0, n_heads + h)),
                  pl.BlockSpec((1, s, hw), lambda bi, h: (bi, 0, h)),
                  small(HEAD_DIM), small(HEAD_DIM), small(HEAD_DIM), small(HEAD_DIM),
                  small(hw)],
        out_specs=pl.BlockSpec((1, s, hw), lambda bi, h: (bi, 0, h)),
        out_shape=jax.ShapeDtypeStruct((b, s, n_heads * hw), jnp.bfloat16),
        scratch_shapes=[pltpu.VMEM((hw + BF16_SUBLANES, s), jnp.bfloat16),
                        pltpu.VMEM((2, s, 2 * tq), jnp.float32)],
        compiler_params=_params(("arbitrary", "arbitrary")),
        name="diff_attn",
    )(qk3, qk3, v3, vec(lq1), vec(lk1), vec(lq2), vec(lk2), vec(gain))


def _out_proj_kernel(p_ref, a_ref, x_ref, w_ref, g_ref, x1_ref, h_ref):
    kp = p_ref.shape[1]
    for rows in _row_subtiles(x_ref.shape[0]):
        y = jnp.dot(p_ref[rows, :], w_ref[:kp, :], preferred_element_type=jnp.float32)
        y += jnp.dot(a_ref[rows, :], w_ref[kp:, :], preferred_element_type=jnp.float32)
        x1_ref[rows, :] = x_ref[rows, :] + y
        h_ref[rows, :] = (_rms_scale(x1_ref[rows, :]) * g_ref[...]).astype(h_ref.dtype)


def _out_proj(pool_out, attn_out, x2d, w_out_bf16, g, tm=512):
    m, d = x2d.shape
    kp, ka = pool_out.shape[1], attn_out.shape[1]
    return pl.pallas_call(
        _out_proj_kernel,
        grid=(m // tm,),
        in_specs=[pl.BlockSpec((tm, kp), lambda i: (i, 0)),
                  pl.BlockSpec((tm, ka), lambda i: (i, 0)),
                  pl.BlockSpec((tm, d), lambda i: (i, 0)),
                  pl.BlockSpec((kp + ka, d), lambda i: (0, 0),
                               pipeline_mode=pl.Buffered(1)),
                  pl.BlockSpec((1, d), lambda i: (0, 0))],
        out_specs=[pl.BlockSpec((tm, d), lambda i: (i, 0)),
                   pl.BlockSpec((tm, d), lambda i: (i, 0))],
        out_shape=[jax.ShapeDtypeStruct((m, d), jnp.float32),
                   jax.ShapeDtypeStruct((m, d), jnp.bfloat16)],
        compiler_params=_params(("arbitrary",)),
        name="out_proj",
    )(pool_out, attn_out, x2d, w_out_bf16, g.reshape(1, d))


def _gate_up_kernel(a_ref, wg_ref, wu_ref, wd_ref, o_ref, wdb_ref, wgb_ref, wub_ref):
    @pl.when(pl.program_id(1) == 0)
    def _():
        wgb_ref[...] = wg_ref[...].astype(jnp.bfloat16)
        wub_ref[...] = wu_ref[...].astype(jnp.bfloat16)

    wdb_ref[...] = wd_ref[...].astype(jnp.bfloat16)

    a = a_ref[...]
    gate = jnp.dot(a, wgb_ref[...], preferred_element_type=jnp.float32)
    up = jnp.dot(a, wub_ref[...], preferred_element_type=jnp.float32)
    o_ref[...] = (gate * jax.nn.sigmoid(gate) * up).astype(o_ref.dtype)


def _gate_up(a, w_gate_up, w_down, tm=1024, th=512):
    m, k = a.shape
    hidden, d = w_down.shape
    nb = hidden // th
    mb = m // tm
    slab, rem = divmod(hidden, nb * mb)
    assert rem == 0 and slab % 16 == 0, "w_down must split into bf16-tileable slabs per step"
    return pl.pallas_call(
        _gate_up_kernel,
        grid=(nb, mb),
        in_specs=[pl.BlockSpec((tm, k), lambda j, i: (i, 0)),
                  pl.BlockSpec((k, th), lambda j, i: (0, j)),
                  pl.BlockSpec((k, th), lambda j, i: (0, j + nb)),
                  pl.BlockSpec((slab, d), lambda j, i: (j * mb + i, 0))],
        out_specs=[pl.BlockSpec((tm, th), lambda j, i: (i, j)),
                   pl.BlockSpec((slab, d), lambda j, i: (j * mb + i, 0))],
        out_shape=[jax.ShapeDtypeStruct((m, hidden), jnp.bfloat16),
                   jax.ShapeDtypeStruct((hidden, d), jnp.bfloat16)],
        scratch_shapes=[pltpu.VMEM((k, th), jnp.bfloat16),
                        pltpu.VMEM((k, th), jnp.bfloat16)],
        compiler_params=_params(("arbitrary", "arbitrary")),
        name="gate_up",
    )(a, w_gate_up, w_gate_up, w_down)


def _down_kernel(a_ref, w_ref, x1_ref, g_ref, o_ref):
    for rows in _row_subtiles(o_ref.shape[0]):
        o_ref[rows, :] = x1_ref[rows, :] + jnp.dot(a_ref[rows, :], w_ref[...],
                                                   preferred_element_type=jnp.float32)
        o_ref[rows, :] = _rms_scale(o_ref[rows, :]) * g_ref[...]


def _down(a, w_down_bf16, x1, g, tm=512):
    m, hidden = a.shape
    d = w_down_bf16.shape[1]
    return pl.pallas_call(
        _down_kernel,
        grid=(m // tm,),
        in_specs=[pl.BlockSpec((tm, hidden), lambda i: (i, 0)),
                  pl.BlockSpec((hidden, d), lambda i: (0, 0), pipeline_mode=pl.Buffered(1)),
                  pl.BlockSpec((tm, d), lambda i: (i, 0)),
                  pl.BlockSpec((1, d), lambda i: (0, 0))],
        out_specs=pl.BlockSpec((tm, d), lambda i: (i, 0)),
        out_shape=jax.ShapeDtypeStruct((m, d), jnp.float32),
        compiler_params=_params(("arbitrary",)),
        name="down",
    )(a, w_down_bf16, x1, g.reshape(1, d))


def _rotary_tables(seq):
    inv_freq = ROPE_THETA ** (-jnp.arange(0, HEAD_DIM, 2, dtype=jnp.float32) / HEAD_DIM)
    ang = jnp.arange(seq, dtype=jnp.float32)[:, None] * inv_freq[None, :]
    cos, sin = jnp.cos(ang), jnp.sin(ang)
    reps = HEAD_WIDTH // HEAD_DIM
    cos_t = jnp.tile(jnp.concatenate([cos, cos], axis=-1), (1, reps))
    sin_t = jnp.tile(jnp.concatenate([-sin, sin], axis=-1), (1, reps))
    return cos_t, sin_t


def kernel(x, norm_mix, w_in, w_pool, pool_scale, lambda_q1, lambda_k1, lambda_q2,
           lambda_k2, subln_gain, w_out, norm_ffn, w_gate_up, w_down, norm_final):
    b, s, d = x.shape
    depth = w_in.shape[0]
    assert depth == 1, "single-layer block"
    pool_width = w_pool.shape[1] * w_pool.shape[2]
    attn_width = w_out.shape[1] - pool_width
    n_heads = attn_width // HEAD_WIDTH

    cos_t, sin_t = _rotary_tables(s)
    x2d = x.reshape(b * s, d)
    l = 0
    h1, pool_out = _norm_pool(x2d, norm_mix[l], w_in[l], w_pool[l], pool_scale[l], s)
    qk_width = n_heads * HEAD_WIDTH
    qk = _in_proj_rope(h1, w_in[l], cos_t, sin_t, col0=pool_width, ncols=2 * qk_width,
                       q_cols=qk_width)
    v, w_out_bf16 = _in_proj_convert(h1, w_in[l], w_out[l], col0=pool_width + 2 * qk_width,
                                     ncols=attn_width)
    attn_out = _attention(qk.reshape(b, s, -1), v.reshape(b, s, -1), lambda_q1[l],
                          lambda_k1[l], lambda_q2[l], lambda_k2[l], subln_gain[l], n_heads)
    x1, h2 = _out_proj(pool_out, attn_out.reshape(b * s, -1), x2d,
                       w_out_bf16, norm_ffn[l])
    a, w_down_bf16 = _gate_up(h2, w_gate_up[l], w_down[l])
    out = _down(a, w_down_bf16, x1, norm_final)
    return out.reshape(b, s, d)
```

```python
import functools
import math

import jax
import jax.numpy as jnp
from jax import lax
from jax.experimental import pallas as pl
from jax.experimental.pallas import tpu as pltpu

CHUNK = 64
POOL_WINDOWS = (2, 4, 8, 16)
HEAD_DIM = 64
HEAD_WIDTH = 2 * HEAD_DIM
ROPE_THETA = 10000.0
NORM_EPS = 1e-6
LAMBDA_INIT = 0.8 - 0.6 * math.exp(-0.3 * 0)

V7X_VMEM_LIMIT_BYTES = 56 * 1024 * 1024

F32_SUBLANES = 8
BF16_SUBLANES = 16

_NEG_BIG = -1e30
LOG2_E = math.log2(math.e)


def _params(semantics):
    return pltpu.CompilerParams(dimension_semantics=semantics,
                                vmem_limit_bytes=V7X_VMEM_LIMIT_BYTES)


def _row_subtiles(rows, n=2):
    step, rem = divmod(rows, n)
    assert rem == 0
    return [slice(r * step, (r + 1) * step) for r in range(n)]


def _rms_scale(x):
    return x * lax.rsqrt(jnp.mean(x * x, axis=-1, keepdims=True) + NORM_EPS)


POOL_HALO = 16


def _shift_rows_down(s, shift):
    if shift == F32_SUBLANES:
        return jnp.concatenate([s[:shift], s[:-shift]], axis=0)
    return pltpu.roll(s, shift, axis=0)


def _pool_group(prev, u, window, is_seq_start):
    s = jnp.concatenate([prev, u], axis=0)
    shift = 1
    while shift < window:
        s = s + _shift_rows_down(s, shift)
        shift *= 2
    s = s[POOL_HALO:]
    pooled = s * (1.0 / window) - u
    cnt = (lax.broadcasted_iota(jnp.int32, (POOL_HALO, u.shape[1]), 0) + 1).astype(jnp.float32)
    cnt = jnp.where(is_seq_start, jnp.minimum(cnt, float(window)), float(window))
    top = s[:POOL_HALO] / cnt - u[:POOL_HALO]
    return jnp.concatenate([top, pooled[POOL_HALO:]], axis=0)


def _norm_pool_kernel(x_ref, g_ref, w_ref, wp_ref, ps_ref, h_ref, o_ref,
                      wbf_ref, wpb_ref, halo_ref, *, tiles_per_seq):
    i = pl.program_id(0)

    @pl.when(i == 0)
    def _():
        wbf_ref[...] = w_ref[...].astype(jnp.bfloat16)
        wpb_ref[...] = wp_ref[...].astype(jnp.bfloat16)
        halo_ref[...] = jnp.zeros_like(halo_ref)

    h = (_rms_scale(x_ref[...]) * g_ref[...]).astype(jnp.bfloat16)
    h_ref[...] = h
    u = jnp.dot(h, wbf_ref[...], preferred_element_type=jnp.float32)

    is_seq_start = (i % tiles_per_seq) == 0
    prev = jnp.where(is_seq_start, 0.0, halo_ref[...])
    halo_ref[...] = u[-POOL_HALO:]
    cg = wp_ref.shape[1]
    for gi, window in enumerate(POOL_WINDOWS):
        cols = slice(gi * cg, (gi + 1) * cg)
        pooled = _pool_group(prev[:, cols], u[:, cols], window, is_seq_start)
        mapped = jnp.dot(pooled.astype(jnp.bfloat16), wpb_ref[gi],
                         preferred_element_type=jnp.float32)
        o_ref[:, cols] = (mapped * ps_ref[:, cols]).astype(o_ref.dtype)


def _norm_pool(x2d, g, w, w_pool, pool_scale, seq, tm=512):
    m, d = x2d.shape
    ng, cg, _ = w_pool.shape
    pw = ng * cg
    assert seq % tm == 0 and max(POOL_WINDOWS) <= POOL_HALO
    return pl.pallas_call(
        functools.partial(_norm_pool_kernel, tiles_per_seq=seq // tm),
        grid=(m // tm,),
        in_specs=[pl.BlockSpec((tm, d), lambda i: (i, 0)),
                  pl.BlockSpec((1, d), lambda i: (0, 0)),
                  pl.BlockSpec((d, pw), lambda i: (0, 0), pipeline_mode=pl.Buffered(1)),
                  pl.BlockSpec((ng, cg, cg), lambda i: (0, 0, 0), pipeline_mode=pl.Buffered(1)),
                  pl.BlockSpec((1, pw), lambda i: (0, 0))],
        out_specs=[pl.BlockSpec((tm, d), lambda i: (i, 0)),
                   pl.BlockSpec((tm, pw), lambda i: (i, 0))],
        out_shape=[jax.ShapeDtypeStruct((m, d), jnp.bfloat16),
                   jax.ShapeDtypeStruct((m, pw), jnp.bfloat16)],
        scratch_shapes=[pltpu.VMEM((d, pw), jnp.bfloat16),
                        pltpu.VMEM((ng, cg, cg), jnp.bfloat16),
                        pltpu.VMEM((POOL_HALO, pw), jnp.float32)],
        compiler_params=_params(("arbitrary",)),
        name="norm_pool",
    )(x2d, g.reshape(1, d), w, w_pool, pool_scale.reshape(1, pw))


def _rope(t, cos_t, sin_t):
    width = t.shape[-1]
    half = HEAD_DIM // 2
    lane = lax.broadcasted_iota(jnp.int32, t.shape, 1)
    first_half = (lane % HEAD_DIM) < half
    partner = jnp.where(first_half,
                        pltpu.roll(t, width - half, axis=1),
                        pltpu.roll(t, half, axis=1))
    return t * cos_t + partner * sin_t


def _in_proj_kernel(a_ref, w_ref, src_ref, o_ref, dst_ref, wbf_ref):
    @pl.when(pl.program_id(1) == 0)
    def _():
        wbf_ref[...] = w_ref[...].astype(jnp.bfloat16)

    dst_ref[...] = src_ref[...].astype(jnp.bfloat16)

    o_ref[...] = jnp.dot(a_ref[...], wbf_ref[...],
                         preferred_element_type=jnp.float32).astype(o_ref.dtype)


def _in_proj_rope_kernel(a_ref, w_ref, cos_ref, sin_ref, src_ref, o_ref, dst_ref, wbf_ref, *,
                         n_q_tiles):
    @pl.when(pl.program_id(1) == 0)
    def _():
        wbf_ref[...] = w_ref[...].astype(jnp.bfloat16)

    dst_ref[...] = src_ref[...].astype(jnp.bfloat16)

    scale = jnp.where(pl.program_id(0) < n_q_tiles, LOG2_E / math.sqrt(HEAD_DIM), 1.0)
    for rows in _row_subtiles(a_ref.shape[0], 4):
        y = jnp.dot(a_ref[rows, :], wbf_ref[...], preferred_element_type=jnp.float32)
        cos_t = cos_ref[rows, :] * scale
        sin_t = sin_ref[rows, :] * scale
        for h in range(y.shape[1] // HEAD_WIDTH):
            cols = slice(h * HEAD_WIDTH, (h + 1) * HEAD_WIDTH)
            o_ref[rows, cols] = _rope(y[:, cols], cos_t, sin_t).astype(o_ref.dtype)


def _in_proj_rope(a, w, cos_t, sin_t, src, *, col0, ncols, q_cols, tm=1024, tn=1024):
    m, k = a.shape
    seq = cos_t.shape[0]
    rows, d = src.shape
    assert col0 % tn == 0 and q_cols % tn == 0 and seq % tm == 0
    jb = col0 // tn
    sb = seq // tm
    nb, mb = ncols // tn, m // tm
    slab, rem = divmod(rows, nb * mb)
    assert rem == 0 and slab % BF16_SUBLANES == 0
    return pl.pallas_call(
        functools.partial(_in_proj_rope_kernel, n_q_tiles=q_cols // tn),
        grid=(nb, mb),
        in_specs=[pl.BlockSpec((tm, k), lambda j, i: (i, 0)),
                  pl.BlockSpec((k, tn), lambda j, i: (0, j + jb)),
                  pl.BlockSpec((tm, HEAD_WIDTH), lambda j, i: (i % sb, 0)),
                  pl.BlockSpec((tm, HEAD_WIDTH), lambda j, i: (i % sb, 0)),
                  pl.BlockSpec((slab, d), lambda j, i: (j * mb + i, 0))],
        out_specs=[pl.BlockSpec((tm, tn), lambda j, i: (i, j)),
                   pl.BlockSpec((slab, d), lambda j, i: (j * mb + i, 0))],
        out_shape=[jax.ShapeDtypeStruct((m, ncols), jnp.bfloat16),
                   jax.ShapeDtypeStruct((rows, d), jnp.bfloat16)],
        scratch_shapes=[pltpu.VMEM((k, tn), jnp.bfloat16)],
        compiler_params=_params(("arbitrary", "arbitrary")),
        name="in_proj_qk",
    )(a, w, cos_t, sin_t, src)


def _in_proj_convert(a, w, src, *, col0, ncols, tm=1024, tn=1024):
    m, k = a.shape
    rows, d = src.shape
    assert col0 % tn == 0
    jb = col0 // tn
    nb, mb = ncols // tn, m // tm
    slab, rem = divmod(rows, nb * mb)
    assert rem == 0 and slab % BF16_SUBLANES == 0
    return pl.pallas_call(
        _in_proj_kernel,
        grid=(nb, mb),
        in_specs=[pl.BlockSpec((tm, k), lambda j, i: (i, 0)),
                  pl.BlockSpec((k, tn), lambda j, i: (0, j + jb)),
                  pl.BlockSpec((slab, d), lambda j, i: (j * mb + i, 0))],
        out_specs=[pl.BlockSpec((tm, tn), lambda j, i: (i, j)),
                   pl.BlockSpec((slab, d), lambda j, i: (j * mb + i, 0))],
        out_shape=[jax.ShapeDtypeStruct((m, ncols), jnp.bfloat16),
                   jax.ShapeDtypeStruct((rows, d), jnp.bfloat16)],
        scratch_shapes=[pltpu.VMEM((k, tn), jnp.bfloat16)],
        compiler_params=_params(("arbitrary", "arbitrary")),
        name="in_proj_v",
    )(a, w, src)


def _attn_kernel(q_ref, k_ref, v_ref, lq1_ref, lk1_ref, lq2_ref, lk2_ref, gain_ref,
                 o_ref, vt_ref, st_even_ref, st_odd_ref, *, tq, kc):
    seq = q_ref.shape[1]
    n_tiles = seq // tq
    hw = HEAD_WIDTH
    st_refs = (st_even_ref, st_odd_ref)

    vt_ref[:hw, :] = v_ref[0].astype(jnp.float32).T.astype(jnp.bfloat16)
    vt_ref[hw:, :] = jnp.ones((vt_ref.shape[0] - hw, seq), jnp.bfloat16)

    lam = (jnp.exp(jnp.sum(lq1_ref[...] * lk1_ref[...], axis=-1, keepdims=True))
           - jnp.exp(jnp.sum(lq2_ref[...] * lk2_ref[...], axis=-1, keepdims=True))
           + LAMBDA_INIT)

    lane = lax.broadcasted_iota(jnp.int32, (tq, hw), 1)
    key = lax.broadcasted_iota(jnp.int32, (kc, 2 * tq), 0)
    qry = lax.broadcasted_iota(jnp.int32, (kc, 2 * tq), 1) % tq

    def n_chunks(i):
        return (i + 1) * tq // kc

    def masked_queries(i):
        q = q_ref[0, i * tq:(i + 1) * tq, :]
        return jnp.concatenate([jnp.where(lane < HEAD_DIM, q, 0),
                                jnp.where(lane >= HEAD_DIM, q, 0)], axis=0)

    def score_chunk(i, c, q2):
        rows = slice(c * kc, (c + 1) * kc)
        s = lax.dot_general(k_ref[0, rows, :], q2, (((1,), (1,)), ((), ())),
                            preferred_element_type=jnp.float32)
        key0 = c * kc - i * tq
        if key0 >= 0:
            s = jnp.where((qry // CHUNK) >= ((key + key0) // CHUNK), s, _NEG_BIG)
        st_refs[i % 2][rows, :] = s
        return jnp.max(s, axis=0, keepdims=True)

    def value_chunk(i, c, m):
        rows = slice(c * kc, (c + 1) * kc)
        p = jnp.exp2(st_refs[i % 2][rows, :] - m).astype(jnp.bfloat16)
        return jnp.dot(vt_ref[:, rows], p, preferred_element_type=jnp.float32)

    def running_max(m, mc):
        return mc if m is None else jnp.maximum(m, mc)

    last = n_tiles - 1
    q2_next = masked_queries(last)
    m_next = None
    for c in range(n_chunks(last)):
        m_next = running_max(m_next, score_chunk(last, c, q2_next))
    for i in range(last, -1, -1):
        m_cur, m_next = m_next, None
        has_next = i > 0
        if has_next:
            q2_next = masked_queries(i - 1)
        acc = None
        for c in range(n_chunks(i)):
            if has_next and c < n_chunks(i - 1):
                m_next = running_max(m_next, score_chunk(i - 1, c, q2_next))
            part = value_chunk(i, c, m_cur)
            acc = part if acc is None else acc + part
        ot = acc[:hw] * (1.0 / acc[hw:hw + 1])
        o = (ot[:, :tq] - lam * ot[:, tq:]).T
        o = _rms_scale(o) * gain_ref[...] * (1.0 - LAMBDA_INIT)
        o_ref[0, i * tq:(i + 1) * tq, :] = o.astype(o_ref.dtype)


def _attention(qk3, v3, lq1, lk1, lq2, lk2, gain, n_heads, tq=256, kc=256):
    b, s, _ = v3.shape
    hw = HEAD_WIDTH
    vec = lambda a: a.reshape(1, -1)
    small = lambda n: pl.BlockSpec((1, n), lambda bi, h: (0, 0))
    return pl.pallas_call(
        functools.partial(_attn_kernel, tq=tq, kc=kc),
        grid=(b, n_heads),
        in_specs=[pl.BlockSpec((1, s, hw), lambda bi, h: (bi, 0, h)),
                  pl.BlockSpec((1, s, hw), lambda bi, h: (bi, 0, n_heads + h)),
                  pl.BlockSpec((1, s, hw), lambda bi, h: (bi, 0, h)),
                  small(HEAD_DIM), small(HEAD_DIM), small(HEAD_DIM), small(HEAD_DIM),
                  small(hw)],
        out_specs=pl.BlockSpec((1, s, hw), lambda bi, h: (bi, 0, h)),
        out_shape=jax.ShapeDtypeStruct((b, s, n_heads * hw), jnp.bfloat16),
        scratch_shapes=[pltpu.VMEM((hw + BF16_SUBLANES, s), jnp.bfloat16),
                        pltpu.VMEM((s, 2 * tq), jnp.float32),
                        pltpu.VMEM((s, 2 * tq), jnp.float32)],
        compiler_params=_params(("arbitrary", "arbitrary")),
        name="diff_attn",
    )(qk3, qk3, v3, vec(lq1), vec(lk1), vec(lq2), vec(lk2), vec(gain))


def _out_proj_kernel(p_ref, a_ref, x_ref, w_ref, g_ref, x1_ref, h_ref):
    kp = p_ref.shape[1]
    for rows in _row_subtiles(x_ref.shape[0]):
        y = jnp.dot(p_ref[rows, :], w_ref[:kp, :], preferred_element_type=jnp.float32)
        y += jnp.dot(a_ref[rows, :], w_ref[kp:, :], preferred_element_type=jnp.float32)
        x1_ref[rows, :] = x_ref[rows, :] + y
        h_ref[rows, :] = (_rms_scale(x1_ref[rows, :]) * g_ref[...]).astype(h_ref.dtype)


def _out_proj(pool_out, attn_out, x2d, w_out_bf16, g, tm=512):
    m, d = x2d.shape
    kp, ka = pool_out.shape[1], attn_out.shape[1]
    return pl.pallas_call(
        _out_proj_kernel,
        grid=(m // tm,),
        in_specs=[pl.BlockSpec((tm, kp), lambda i: (i, 0)),
                  pl.BlockSpec((tm, ka), lambda i: (i, 0)),
                  pl.BlockSpec((tm, d), lambda i: (i, 0)),
                  pl.BlockSpec((kp + ka, d), lambda i: (0, 0),
                               pipeline_mode=pl.Buffered(1)),
                  pl.BlockSpec((1, d), lambda i: (0, 0))],
        out_specs=[pl.BlockSpec((tm, d), lambda i: (i, 0)),
                   pl.BlockSpec((tm, d), lambda i: (i, 0))],
        out_shape=[jax.ShapeDtypeStruct((m, d), jnp.float32),
                   jax.ShapeDtypeStruct((m, d), jnp.bfloat16)],
        compiler_params=_params(("arbitrary",)),
        name="out_proj",
    )(pool_out, attn_out, x2d, w_out_bf16, g.reshape(1, d))


def _gate_up_kernel(a_ref, wg_ref, wu_ref, o_ref, wgb_ref, wub_ref):
    @pl.when(pl.program_id(1) == 0)
    def _():
        wgb_ref[...] = wg_ref[...].astype(jnp.bfloat16)
        wub_ref[...] = wu_ref[...].astype(jnp.bfloat16)

    for rows in _row_subtiles(a_ref.shape[0]):
        a = a_ref[rows, :]
        gate = jnp.dot(a, wgb_ref[...], preferred_element_type=jnp.float32)
        up = jnp.dot(a, wub_ref[...], preferred_element_type=jnp.float32)
        o_ref[rows, :] = (gate * jax.nn.sigmoid(gate) * up).astype(o_ref.dtype)


def _gate_up(a, w_gate_up, hidden, tm=2048, th=512):
    m, k = a.shape
    nb = hidden // th
    return pl.pallas_call(
        _gate_up_kernel,
        grid=(nb, m // tm),
        in_specs=[pl.BlockSpec((tm, k), lambda j, i: (i, 0)),
                  pl.BlockSpec((k, th), lambda j, i: (0, j)),
                  pl.BlockSpec((k, th), lambda j, i: (0, j + nb))],
        out_specs=pl.BlockSpec((tm, th), lambda j, i: (i, j)),
        out_shape=jax.ShapeDtypeStruct((m, hidden), jnp.bfloat16),
        scratch_shapes=[pltpu.VMEM((k, th), jnp.bfloat16),
                        pltpu.VMEM((k, th), jnp.bfloat16)],
        compiler_params=_params(("arbitrary", "arbitrary")),
        name="gate_up",
    )(a, w_gate_up, w_gate_up)


def _down_kernel(a_ref, w_ref, x1_ref, g_ref, o_ref):
    for rows in _row_subtiles(o_ref.shape[0]):
        o_ref[rows, :] = x1_ref[rows, :] + jnp.dot(a_ref[rows, :], w_ref[...],
                                                   preferred_element_type=jnp.float32)
        o_ref[rows, :] = _rms_scale(o_ref[rows, :]) * g_ref[...]


def _down(a, w_down_bf16, x1, g, tm=512):
    m, hidden = a.shape
    d = w_down_bf16.shape[1]
    return pl.pallas_call(
        _down_kernel,
        grid=(m // tm,),
        in_specs=[pl.BlockSpec((tm, hidden), lambda i: (i, 0)),
                  pl.BlockSpec((hidden, d), lambda i: (0, 0), pipeline_mode=pl.Buffered(1)),
                  pl.BlockSpec((tm, d), lambda i: (i, 0)),
                  pl.BlockSpec((1, d), lambda i: (0, 0))],
        out_specs=pl.BlockSpec((tm, d), lambda i: (i, 0)),
        out_shape=jax.ShapeDtypeStruct((m, d), jnp.float32),
        compiler_params=_params(("arbitrary",)),
        name="down",
    )(a, w_down_bf16, x1, g.reshape(1, d))


def _rotary_tables(seq):
    inv_freq = ROPE_THETA ** (-jnp.arange(0, HEAD_DIM, 2, dtype=jnp.float32) / HEAD_DIM)
    ang = jnp.arange(seq, dtype=jnp.float32)[:, None] * inv_freq[None, :]
    cos, sin = jnp.cos(ang), jnp.sin(ang)
    reps = HEAD_WIDTH // HEAD_DIM
    cos_t = jnp.tile(jnp.concatenate([cos, cos], axis=-1), (1, reps))
    sin_t = jnp.tile(jnp.concatenate([-sin, sin], axis=-1), (1, reps))
    return cos_t, sin_t


def kernel(x, norm_mix, w_in, w_pool, pool_scale, lambda_q1, lambda_k1, lambda_q2,
           lambda_k2, subln_gain, w_out, norm_ffn, w_gate_up, w_down, norm_final):
    b, s, d = x.shape
    depth = w_in.shape[0]
    assert depth == 1, "single-layer block"
    pool_width = w_pool.shape[1] * w_pool.shape[2]
    attn_width = w_out.shape[1] - pool_width
    n_heads = attn_width // HEAD_WIDTH

    cos_t, sin_t = _rotary_tables(s)
    x2d = x.reshape(b * s, d)
    l = 0
    h1, pool_out = _norm_pool(x2d, norm_mix[l], w_in[l], w_pool[l], pool_scale[l], s)
    qk_width = n_heads * HEAD_WIDTH
    qk, w_down_bf16 = _in_proj_rope(h1, w_in[l], cos_t, sin_t, w_down[l], col0=pool_width,
                                    ncols=2 * qk_width, q_cols=qk_width)
    v, w_out_bf16 = _in_proj_convert(h1, w_in[l], w_out[l], col0=pool_width + 2 * qk_width,
                                     ncols=attn_width)
    attn_out = _attention(qk.reshape(b, s, -1), v.reshape(b, s, -1), lambda_q1[l],
                          lambda_k1[l], lambda_q2[l], lambda_k2[l], subln_gain[l], n_heads)
    x1, h2 = _out_proj(pool_out, attn_out.reshape(b * s, -1), x2d,
                       w_out_bf16, norm_ffn[l])
    a = _gate_up(h2, w_gate_up[l], w_down.shape[1])
    out = _down(a, w_down_bf16, x1, norm_final)
    return out.reshape(b, s, d)
```

```python
import functools
import math

import jax
import jax.numpy as jnp
from jax import lax
from jax.experimental import pallas as pl
from jax.experimental.pallas import tpu as pltpu

CHUNK = 64
POOL_WINDOWS = (2, 4, 8, 16)
HEAD_DIM = 64
HEAD_WIDTH = 2 * HEAD_DIM
ROPE_THETA = 10000.0
NORM_EPS = 1e-6
LAMBDA_INIT = 0.8 - 0.6 * math.exp(-0.3 * 0)

V7X_VMEM_LIMIT_BYTES = 56 * 1024 * 1024

F32_SUBLANES = 8
BF16_SUBLANES = 16

_NEG_BIG = -1e30
LOG2_E = math.log2(math.e)


def _params(semantics):
    return pltpu.CompilerParams(dimension_semantics=semantics,
                                vmem_limit_bytes=V7X_VMEM_LIMIT_BYTES)


def _row_subtiles(rows, n=2):
    step, rem = divmod(rows, n)
    assert rem == 0
    return [slice(r * step, (r + 1) * step) for r in range(n)]


def _rms_scale(x):
    return x * lax.rsqrt(jnp.mean(x * x, axis=-1, keepdims=True) + NORM_EPS)


POOL_HALO = 16


def _shift_rows_down(s, shift):
    if shift == F32_SUBLANES:
        return jnp.concatenate([s[:shift], s[:-shift]], axis=0)
    return pltpu.roll(s, shift, axis=0)


def _pool_group(prev, u, window, is_seq_start):
    s = jnp.concatenate([prev, u], axis=0)
    shift = 1
    while shift < window:
        s = s + _shift_rows_down(s, shift)
        shift *= 2
    s = s[POOL_HALO:]
    pooled = s * (1.0 / window) - u
    cnt = (lax.broadcasted_iota(jnp.int32, (POOL_HALO, u.shape[1]), 0) + 1).astype(jnp.float32)
    cnt = jnp.where(is_seq_start, jnp.minimum(cnt, float(window)), float(window))
    top = s[:POOL_HALO] / cnt - u[:POOL_HALO]
    return jnp.concatenate([top, pooled[POOL_HALO:]], axis=0)


def _norm_pool_kernel(x_ref, g_ref, w_ref, wp_ref, ps_ref, h_ref, o_ref,
                      wbf_ref, wpb_ref, halo_ref, *, tiles_per_seq):
    i = pl.program_id(0)

    @pl.when(i == 0)
    def _():
        wbf_ref[...] = w_ref[...].astype(jnp.bfloat16)
        wpb_ref[...] = wp_ref[...].astype(jnp.bfloat16)
        halo_ref[...] = jnp.zeros_like(halo_ref)

    h = (_rms_scale(x_ref[...]) * g_ref[...]).astype(jnp.bfloat16)
    h_ref[...] = h
    u = jnp.dot(h, wbf_ref[...], preferred_element_type=jnp.float32)

    is_seq_start = (i % tiles_per_seq) == 0
    prev = jnp.where(is_seq_start, 0.0, halo_ref[...])
    halo_ref[...] = u[-POOL_HALO:]
    cg = wp_ref.shape[1]
    for gi, window in enumerate(POOL_WINDOWS):
        cols = slice(gi * cg, (gi + 1) * cg)
        pooled = _pool_group(prev[:, cols], u[:, cols], window, is_seq_start)
        mapped = jnp.dot(pooled.astype(jnp.bfloat16), wpb_ref[gi],
                         preferred_element_type=jnp.float32)
        o_ref[:, cols] = (mapped * ps_ref[:, cols]).astype(o_ref.dtype)


def _norm_pool(x2d, g, w, w_pool, pool_scale, seq, tm=512):
    m, d = x2d.shape
    ng, cg, _ = w_pool.shape
    pw = ng * cg
    assert seq % tm == 0 and max(POOL_WINDOWS) <= POOL_HALO
    return pl.pallas_call(
        functools.partial(_norm_pool_kernel, tiles_per_seq=seq // tm),
        grid=(m // tm,),
        in_specs=[pl.BlockSpec((tm, d), lambda i: (i, 0)),
                  pl.BlockSpec((1, d), lambda i: (0, 0)),
                  pl.BlockSpec((d, pw), lambda i: (0, 0), pipeline_mode=pl.Buffered(1)),
                  pl.BlockSpec((ng, cg, cg), lambda i: (0, 0, 0), pipeline_mode=pl.Buffered(1)),
                  pl.BlockSpec((1, pw), lambda i: (0, 0))],
        out_specs=[pl.BlockSpec((tm, d), lambda i: (i, 0)),
                   pl.BlockSpec((tm, pw), lambda i: (i, 0))],
        out_shape=[jax.ShapeDtypeStruct((m, d), jnp.bfloat16),
                   jax.ShapeDtypeStruct((m, pw), jnp.bfloat16)],
        scratch_shapes=[pltpu.VMEM((d, pw), jnp.bfloat16),
                        pltpu.VMEM((ng, cg, cg), jnp.bfloat16),
                        pltpu.VMEM((POOL_HALO, pw), jnp.float32)],
        compiler_params=_params(("arbitrary",)),
        name="norm_pool",
    )(x2d, g.reshape(1, d), w, w_pool, pool_scale.reshape(1, pw))


def _rope(t, cos_t, sin_t):
    width = t.shape[-1]
    half = HEAD_DIM // 2
    lane = lax.broadcasted_iota(jnp.int32, t.shape, 1)
    first_half = (lane % HEAD_DIM) < half
    partner = jnp.where(first_half,
                        pltpu.roll(t, width - half, axis=1),
                        pltpu.roll(t, half, axis=1))
    return t * cos_t + partner * sin_t


def _in_proj_kernel(a_ref, w_ref, src_ref, o_ref, dst_ref, wbf_ref):
    @pl.when(pl.program_id(1) == 0)
    def _():
        wbf_ref[...] = w_ref[...].astype(jnp.bfloat16)

    dst_ref[...] = src_ref[...].astype(jnp.bfloat16)

    o_ref[...] = jnp.dot(a_ref[...], wbf_ref[...],
                         preferred_element_type=jnp.float32).astype(o_ref.dtype)


def _in_proj_rope_kernel(a_ref, w_ref, cos_ref, sin_ref, src_ref, o_ref, dst_ref, wbf_ref, *,
                         n_q_tiles):
    @pl.when(pl.program_id(1) == 0)
    def _():
        wbf_ref[...] = w_ref[...].astype(jnp.bfloat16)

    dst_ref[...] = src_ref[...].astype(jnp.bfloat16)

    scale = jnp.where(pl.program_id(0) < n_q_tiles, LOG2_E / math.sqrt(HEAD_DIM), 1.0)
    for rows in _row_subtiles(a_ref.shape[0], 4):
        y = jnp.dot(a_ref[rows, :], wbf_ref[...], preferred_element_type=jnp.float32)
        cos_t = cos_ref[rows, :] * scale
        sin_t = sin_ref[rows, :] * scale
        for h in range(y.shape[1] // HEAD_WIDTH):
            cols = slice(h * HEAD_WIDTH, (h + 1) * HEAD_WIDTH)
            o_ref[rows, cols] = _rope(y[:, cols], cos_t, sin_t).astype(o_ref.dtype)


def _in_proj_rope(a, w, cos_t, sin_t, src, *, col0, ncols, q_cols, tm=1024, tn=1024):
    m, k = a.shape
    seq = cos_t.shape[0]
    rows, d = src.shape
    assert col0 % tn == 0 and q_cols % tn == 0 and seq % tm == 0
    jb = col0 // tn
    sb = seq // tm
    nb, mb = ncols // tn, m // tm
    slab, rem = divmod(rows, nb * mb)
    assert rem == 0 and slab % BF16_SUBLANES == 0
    return pl.pallas_call(
        functools.partial(_in_proj_rope_kernel, n_q_tiles=q_cols // tn),
        grid=(nb, mb),
        in_specs=[pl.BlockSpec((tm, k), lambda j, i: (i, 0)),
                  pl.BlockSpec((k, tn), lambda j, i: (0, j + jb)),
                  pl.BlockSpec((tm, HEAD_WIDTH), lambda j, i: (i % sb, 0)),
                  pl.BlockSpec((tm, HEAD_WIDTH), lambda j, i: (i % sb, 0)),
                  pl.BlockSpec((slab, d), lambda j, i: (j * mb + i, 0))],
        out_specs=[pl.BlockSpec((tm, tn), lambda j, i: (i, j)),
                   pl.BlockSpec((slab, d), lambda j, i: (j * mb + i, 0))],
        out_shape=[jax.ShapeDtypeStruct((m, ncols), jnp.bfloat16),
                   jax.ShapeDtypeStruct((rows, d), jnp.bfloat16)],
        scratch_shapes=[pltpu.VMEM((k, tn), jnp.bfloat16)],
        compiler_params=_params(("arbitrary", "arbitrary")),
        name="in_proj_qk",
    )(a, w, cos_t, sin_t, src)


def _in_proj_convert(a, w, src, *, col0, ncols, tm=1024, tn=1024):
    m, k = a.shape
    rows, d = src.shape
    assert col0 % tn == 0
    jb = col0 // tn
    nb, mb = ncols // tn, m // tm
    slab, rem = divmod(rows, nb * mb)
    assert rem == 0 and slab % BF16_SUBLANES == 0
    return pl.pallas_call(
        _in_proj_kernel,
        grid=(nb, mb),
        in_specs=[pl.BlockSpec((tm, k), lambda j, i: (i, 0)),
                  pl.BlockSpec((k, tn), lambda j, i: (0, j + jb)),
                  pl.BlockSpec((slab, d), lambda j, i: (j * mb + i, 0))],
        out_specs=[pl.BlockSpec((tm, tn), lambda j, i: (i, j)),
                   pl.BlockSpec((slab, d), lambda j, i: (j * mb + i, 0))],
        out_shape=[jax.ShapeDtypeStruct((m, ncols), jnp.bfloat16),
                   jax.ShapeDtypeStruct((rows, d), jnp.bfloat16)],
        scratch_shapes=[pltpu.VMEM((k, tn), jnp.bfloat16)],
        compiler_params=_params(("arbitrary", "arbitrary")),
        name="in_proj_v",
    )(a, w, src)


def _attn_kernel(q_ref, k_ref, v_ref, lq1_ref, lk1_ref, lq2_ref, lk2_ref, gain_ref,
                 o_ref, vt_ref, *st_refs, tq):
    seq = q_ref.shape[1]
    n_tiles = seq // tq
    kc = tq
    hw = HEAD_WIDTH

    vt_ref[:hw, :] = v_ref[0].astype(jnp.float32).T.astype(jnp.bfloat16)
    vt_ref[hw:, :] = jnp.ones((vt_ref.shape[0] - hw, seq), jnp.bfloat16)

    lam = (jnp.exp(jnp.sum(lq1_ref[...] * lk1_ref[...], axis=-1, keepdims=True))
           - jnp.exp(jnp.sum(lq2_ref[...] * lk2_ref[...], axis=-1, keepdims=True))
           + LAMBDA_INIT)

    lane = lax.broadcasted_iota(jnp.int32, (tq, hw), 1)
    key = lax.broadcasted_iota(jnp.int32, (kc, 2 * tq), 0)
    qry = lax.broadcasted_iota(jnp.int32, (kc, 2 * tq), 1) % tq

    def masked_queries(i):
        q = q_ref[0, i * tq:(i + 1) * tq, :]
        return jnp.concatenate([jnp.where(lane < HEAD_DIM, q, 0),
                                jnp.where(lane >= HEAD_DIM, q, 0)], axis=0)

    def scores(i, c, q2, slot):
        rows = slice(c * kc, (c + 1) * kc)
        s = lax.dot_general(k_ref[0, rows, :], q2, (((1,), (1,)), ((), ())),
                            preferred_element_type=jnp.float32)
        if c == i:
            s = jnp.where((qry // CHUNK) >= (key // CHUNK), s, -jnp.inf)
        st_refs[slot][...] = s
        return jnp.max(s, axis=0, keepdims=True)

    def values(c, m_c, slot):
        rows = slice(c * kc, (c + 1) * kc)
        p = jnp.exp2(st_refs[slot][...] - m_c).astype(jnp.bfloat16)
        return jnp.dot(vt_ref[:, rows], p, preferred_element_type=jnp.float32)

    items = [(i, c) for i in range(n_tiles - 1, -1, -1) for c in range(i + 1)]
    lookahead = len(st_refs) - 1
    q2s, m_cs = {}, {}

    def emit_scores(n):
        i, c = items[n]
        if i not in q2s:
            q2s[i] = masked_queries(i)
        m_cs[n] = scores(i, c, q2s[i], n % len(st_refs))

    for n in range(min(lookahead, len(items))):
        emit_scores(n)
    m_run = acc = None
    for n, (i, c) in enumerate(items):
        if n + lookahead < len(items):
            emit_scores(n + lookahead)
        m_c = m_cs.pop(n)
        part = values(c, m_c, n % len(st_refs))
        if c == 0:
            m_run, acc = m_c, part
        else:
            m_new = jnp.maximum(m_run, m_c)
            acc = acc * jnp.exp2(m_run - m_new) + part * jnp.exp2(m_c - m_new)
            m_run = m_new
        if c < i:
            continue
        ot = acc[:hw] * (1.0 / acc[hw:hw + 1])
        o = (ot[:, :tq] - lam * ot[:, tq:]).T
        o = _rms_scale(o) * gain_ref[...] * (1.0 - LAMBDA_INIT)
        o_ref[0, i * tq:(i + 1) * tq, :] = o.astype(o_ref.dtype)


def _attention(qk3, v3, lq1, lk1, lq2, lk2, gain, n_heads, tq=256, n_slots=3):
    b, s, _ = v3.shape
    hw = HEAD_WIDTH
    vec = lambda a: a.reshape(1, -1)
    small = lambda n: pl.BlockSpec((1, n), lambda bi, h: (0, 0))
    return pl.pallas_call(
        functools.partial(_attn_kernel, tq=tq),
        grid=(b, n_heads),
        in_specs=[pl.BlockSpec((1, s, hw), lambda bi, h: (bi, 0, h)),
                  pl.BlockSpec((1, s, hw), lambda bi, h: (bi, 0, n_heads + h)),
                  pl.BlockSpec((1, s, hw), lambda bi, h: (bi, 0, h)),
                  small(HEAD_DIM), small(HEAD_DIM), small(HEAD_DIM), small(HEAD_DIM),
                  small(hw)],
        out_specs=pl.BlockSpec((1, s, hw), lambda bi, h: (bi, 0, h)),
        out_shape=jax.ShapeDtypeStruct((b, s, n_heads * hw), jnp.bfloat16),
        scratch_shapes=[pltpu.VMEM((hw + BF16_SUBLANES, s), jnp.bfloat16)]
        + [pltpu.VMEM((tq, 2 * tq), jnp.float32)] * n_slots,
        compiler_params=_params(("arbitrary", "arbitrary")),
        name="diff_attn",
    )(qk3, qk3, v3, vec(lq1), vec(lk1), vec(lq2), vec(lk2), vec(gain))


def _out_proj_kernel(p_ref, a_ref, x_ref, w_ref, g_ref, x1_ref, h_ref):
    kp = p_ref.shape[1]
    for rows in _row_subtiles(x_ref.shape[0]):
        y = jnp.dot(p_ref[rows, :], w_ref[:kp, :], preferred_element_type=jnp.float32)
        y += jnp.dot(a_ref[rows, :], w_ref[kp:, :], preferred_element_type=jnp.float32)
        x1_ref[rows, :] = x_ref[rows, :] + y
        h_ref[rows, :] = (_rms_scale(x1_ref[rows, :]) * g_ref[...]).astype(h_ref.dtype)


def _out_proj(pool_out, attn_out, x2d, w_out_bf16, g, tm=512):
    m, d = x2d.shape
    kp, ka = pool_out.shape[1], attn_out.shape[1]
    return pl.pallas_call(
        _out_proj_kernel,
        grid=(m // tm,),
        in_specs=[pl.BlockSpec((tm, kp), lambda i: (i, 0)),
                  pl.BlockSpec((tm, ka), lambda i: (i, 0)),
                  pl.BlockSpec((tm, d), lambda i: (i, 0)),
                  pl.BlockSpec((kp + ka, d), lambda i: (0, 0),
                               pipeline_mode=pl.Buffered(1)),
                  pl.BlockSpec((1, d), lambda i: (0, 0))],
        out_specs=[pl.BlockSpec((tm, d), lambda i: (i, 0)),
                   pl.BlockSpec((tm, d), lambda i: (i, 0))],
        out_shape=[jax.ShapeDtypeStruct((m, d), jnp.float32),
                   jax.ShapeDtypeStruct((m, d), jnp.bfloat16)],
        compiler_params=_params(("arbitrary",)),
        name="out_proj",
    )(pool_out, attn_out, x2d, w_out_bf16, g.reshape(1, d))


def _gate_up_kernel(a_ref, wg_ref, wu_ref, o_ref, wgb_ref, wub_ref):
    @pl.when(pl.program_id(1) == 0)
    def _():
        wgb_ref[...] = wg_ref[...].astype(jnp.bfloat16)
        wub_ref[...] = wu_ref[...].astype(jnp.bfloat16)

    for rows in _row_subtiles(a_ref.shape[0]):
        a = a_ref[rows, :]
        gate = jnp.dot(a, wgb_ref[...], preferred_element_type=jnp.float32)
        up = jnp.dot(a, wub_ref[...], preferred_element_type=jnp.float32)
        o_ref[rows, :] = (gate * jax.nn.sigmoid(gate) * up).astype(o_ref.dtype)


def _gate_up(a, w_gate_up, hidden, tm=2048, th=512):
    m, k = a.shape
    nb = hidden // th
    return pl.pallas_call(
        _gate_up_kernel,
        grid=(nb, m // tm),
        in_specs=[pl.BlockSpec((tm, k), lambda j, i: (i, 0)),
                  pl.BlockSpec((k, th), lambda j, i: (0, j)),
                  pl.BlockSpec((k, th), lambda j, i: (0, j + nb))],
        out_specs=pl.BlockSpec((tm, th), lambda j, i: (i, j)),
        out_shape=jax.ShapeDtypeStruct((m, hidden), jnp.bfloat16),
        scratch_shapes=[pltpu.VMEM((k, th), jnp.bfloat16),
                        pltpu.VMEM((k, th), jnp.bfloat16)],
        compiler_params=_params(("arbitrary", "arbitrary")),
        name="gate_up",
    )(a, w_gate_up, w_gate_up)


def _down_kernel(a_ref, w_ref, x1_ref, g_ref, o_ref):
    for rows in _row_subtiles(o_ref.shape[0]):
        o_ref[rows, :] = x1_ref[rows, :] + jnp.dot(a_ref[rows, :], w_ref[...],
                                                   preferred_element_type=jnp.float32)
        o_ref[rows, :] = _rms_scale(o_ref[rows, :]) * g_ref[...]


def _down(a, w_down_bf16, x1, g, tm=512):
    m, hidden = a.shape
    d = w_down_bf16.shape[1]
    return pl.pallas_call(
        _down_kernel,
        grid=(m // tm,),
        in_specs=[pl.BlockSpec((tm, hidden), lambda i: (i, 0)),
                  pl.BlockSpec((hidden, d), lambda i: (0, 0), pipeline_mode=pl.Buffered(1)),
                  pl.BlockSpec((tm, d), lambda i: (i, 0)),
                  pl.BlockSpec((1, d), lambda i: (0, 0))],
        out_specs=pl.BlockSpec((tm, d), lambda i: (i, 0)),
        out_shape=jax.ShapeDtypeStruct((m, d), jnp.float32),
        compiler_params=_params(("arbitrary",)),
        name="down",
    )(a, w_down_bf16, x1, g.reshape(1, d))


def _rotary_tables(seq):
    inv_freq = ROPE_THETA ** (-jnp.arange(0, HEAD_DIM, 2, dtype=jnp.float32) / HEAD_DIM)
    ang = jnp.arange(seq, dtype=jnp.float32)[:, None] * inv_freq[None, :]
    cos, sin = jnp.cos(ang), jnp.sin(ang)
    reps = HEAD_WIDTH // HEAD_DIM
    cos_t = jnp.tile(jnp.concatenate([cos, cos], axis=-1), (1, reps))
    sin_t = jnp.tile(jnp.concatenate([-sin, sin], axis=-1), (1, reps))
    return cos_t, sin_t


def kernel(x, norm_mix, w_in, w_pool, pool_scale, lambda_q1, lambda_k1, lambda_q2,
           lambda_k2, subln_gain, w_out, norm_ffn, w_gate_up, w_down, norm_final):
    b, s, d = x.shape
    depth = w_in.shape[0]
    assert depth == 1, "single-layer block"
    pool_width = w_pool.shape[1] * w_pool.shape[2]
    attn_width = w_out.shape[1] - pool_width
    n_heads = attn_width // HEAD_WIDTH

    cos_t, sin_t = _rotary_tables(s)
    x2d = x.reshape(b * s, d)
    l = 0
    h1, pool_out = _norm_pool(x2d, norm_mix[l], w_in[l], w_pool[l], pool_scale[l], s)
    qk_width = n_heads * HEAD_WIDTH
    qk, w_down_bf16 = _in_proj_rope(h1, w_in[l], cos_t, sin_t, w_down[l], col0=pool_width,
                                    ncols=2 * qk_width, q_cols=qk_width)
    v, w_out_bf16 = _in_proj_convert(h1, w_in[l], w_out[l], col0=pool_width + 2 * qk_width,
                                     ncols=attn_width)
    attn_out = _attention(qk.reshape(b, s, -1), v.reshape(b, s, -1), lambda_q1[l],
                          lambda_k1[l], lambda_q2[l], lambda_k2[l], subln_gain[l], n_heads)
    x1, h2 = _out_proj(pool_out, attn_out.reshape(b * s, -1), x2d,
                       w_out_bf16, norm_ffn[l])
    a = _gate_up(h2, w_gate_up[l], w_down.shape[1])
    out = _down(a, w_down_bf16, x1, norm_final)
    return out.reshape(b, s, d)
```

```python
import functools
import math

import jax
import jax.numpy as jnp
from jax import lax
from jax.experimental import pallas as pl
from jax.experimental.pallas import tpu as pltpu

CHUNK = 64
POOL_WINDOWS = (2, 4, 8, 16)
HEAD_DIM = 64
HEAD_WIDTH = 2 * HEAD_DIM
ROPE_THETA = 10000.0
NORM_EPS = 1e-6
LAMBDA_INIT = 0.8 - 0.6 * math.exp(-0.3 * 0)

V7X_VMEM_LIMIT_BYTES = 56 * 1024 * 1024

F32_SUBLANES = 8
BF16_SUBLANES = 16

_NEG_BIG = -1e30
LOG2_E = math.log2(math.e)


def _params(semantics):
    return pltpu.CompilerParams(dimension_semantics=semantics,
                                vmem_limit_bytes=V7X_VMEM_LIMIT_BYTES)


def _row_subtiles(rows, n=2):
    step, rem = divmod(rows, n)
    assert rem == 0
    return [slice(r * step, (r + 1) * step) for r in range(n)]


def _rms_scale(x):
    return x * lax.rsqrt(jnp.mean(x * x, axis=-1, keepdims=True) + NORM_EPS)


POOL_HALO = 16


def _shift_rows_down(s, shift):
    if shift == F32_SUBLANES:
        return jnp.concatenate([s[:shift], s[:-shift]], axis=0)
    return pltpu.roll(s, shift, axis=0)


def _pool_group(prev, u, window, is_seq_start):
    s = jnp.concatenate([prev, u], axis=0)
    shift = 1
    while shift < window:
        s = s + _shift_rows_down(s, shift)
        shift *= 2
    s = s[POOL_HALO:]
    pooled = s * (1.0 / window) - u
    cnt = (lax.broadcasted_iota(jnp.int32, (POOL_HALO, u.shape[1]), 0) + 1).astype(jnp.float32)
    cnt = jnp.where(is_seq_start, jnp.minimum(cnt, float(window)), float(window))
    top = s[:POOL_HALO] / cnt - u[:POOL_HALO]
    return jnp.concatenate([top, pooled[POOL_HALO:]], axis=0)


def _norm_pool_kernel(x_ref, g_ref, w_ref, wp_ref, ps_ref, h_ref, o_ref,
                      wbf_ref, wpb_ref, halo_ref, *, tiles_per_seq):
    i = pl.program_id(0)

    @pl.when(i == 0)
    def _():
        wbf_ref[...] = w_ref[...].astype(jnp.bfloat16)
        wpb_ref[...] = wp_ref[...].astype(jnp.bfloat16)
        halo_ref[...] = jnp.zeros_like(halo_ref)

    h = (_rms_scale(x_ref[...]) * g_ref[...]).astype(jnp.bfloat16)
    h_ref[...] = h
    u = jnp.dot(h, wbf_ref[...], preferred_element_type=jnp.float32)

    is_seq_start = (i % tiles_per_seq) == 0
    prev = jnp.where(is_seq_start, 0.0, halo_ref[...])
    halo_ref[...] = u[-POOL_HALO:]
    cg = wp_ref.shape[1]
    for gi, window in enumerate(POOL_WINDOWS):
        cols = slice(gi * cg, (gi + 1) * cg)
        pooled = _pool_group(prev[:, cols], u[:, cols], window, is_seq_start)
        mapped = jnp.dot(pooled.astype(jnp.bfloat16), wpb_ref[gi],
                         preferred_element_type=jnp.float32)
        o_ref[:, cols] = (mapped * ps_ref[:, cols]).astype(o_ref.dtype)


def _norm_pool(x2d, g, w, w_pool, pool_scale, seq, tm=512):
    m, d = x2d.shape
    ng, cg, _ = w_pool.shape
    pw = ng * cg
    assert seq % tm == 0 and max(POOL_WINDOWS) <= POOL_HALO
    return pl.pallas_call(
        functools.partial(_norm_pool_kernel, tiles_per_seq=seq // tm),
        grid=(m // tm,),
        in_specs=[pl.BlockSpec((tm, d), lambda i: (i, 0)),
                  pl.BlockSpec((1, d), lambda i: (0, 0)),
                  pl.BlockSpec((d, pw), lambda i: (0, 0), pipeline_mode=pl.Buffered(1)),
                  pl.BlockSpec((ng, cg, cg), lambda i: (0, 0, 0), pipeline_mode=pl.Buffered(1)),
                  pl.BlockSpec((1, pw), lambda i: (0, 0))],
        out_specs=[pl.BlockSpec((tm, d), lambda i: (i, 0)),
                   pl.BlockSpec((tm, pw), lambda i: (i, 0))],
        out_shape=[jax.ShapeDtypeStruct((m, d), jnp.bfloat16),
                   jax.ShapeDtypeStruct((m, pw), jnp.bfloat16)],
        scratch_shapes=[pltpu.VMEM((d, pw), jnp.bfloat16),
                        pltpu.VMEM((ng, cg, cg), jnp.bfloat16),
                        pltpu.VMEM((POOL_HALO, pw), jnp.float32)],
        compiler_params=_params(("arbitrary",)),
        name="norm_pool",
    )(x2d, g.reshape(1, d), w, w_pool, pool_scale.reshape(1, pw))


def _rope(t, cos_t, sin_t):
    width = t.shape[-1]
    half = HEAD_DIM // 2
    lane = lax.broadcasted_iota(jnp.int32, t.shape, 1)
    first_half = (lane % HEAD_DIM) < half
    partner = jnp.where(first_half,
                        pltpu.roll(t, width - half, axis=1),
                        pltpu.roll(t, half, axis=1))
    return t * cos_t + partner * sin_t


def _in_proj_kernel(a_ref, w_ref, src_ref, o_ref, dst_ref, wbf_ref):
    @pl.when(pl.program_id(1) == 0)
    def _():
        wbf_ref[...] = w_ref[...].astype(jnp.bfloat16)

    dst_ref[...] = src_ref[...].astype(jnp.bfloat16)

    o_ref[...] = jnp.dot(a_ref[...], wbf_ref[...],
                         preferred_element_type=jnp.float32).astype(o_ref.dtype)


def _in_proj_rope_kernel(a_ref, w_ref, cos_ref, sin_ref, src_ref, o_ref, dst_ref, wbf_ref, *,
                         n_q_tiles):
    @pl.when(pl.program_id(1) == 0)
    def _():
        wbf_ref[...] = w_ref[...].astype(jnp.bfloat16)

    dst_ref[...] = src_ref[...].astype(jnp.bfloat16)

    scale = jnp.where(pl.program_id(0) < n_q_tiles, LOG2_E / math.sqrt(HEAD_DIM), 1.0)
    for rows in _row_subtiles(a_ref.shape[0], 4):
        y = jnp.dot(a_ref[rows, :], wbf_ref[...], preferred_element_type=jnp.float32)
        cos_t = cos_ref[rows, :] * scale
        sin_t = sin_ref[rows, :] * scale
        for h in range(y.shape[1] // HEAD_WIDTH):
            cols = slice(h * HEAD_WIDTH, (h + 1) * HEAD_WIDTH)
            o_ref[rows, cols] = _rope(y[:, cols], cos_t, sin_t).astype(o_ref.dtype)


def _in_proj_rope(a, w, cos_t, sin_t, src, *, col0, ncols, q_cols, tm=1024, tn=1024):
    m, k = a.shape
    seq = cos_t.shape[0]
    rows, d = src.shape
    assert col0 % tn == 0 and q_cols % tn == 0 and seq % tm == 0
    jb = col0 // tn
    sb = seq // tm
    nb, mb = ncols // tn, m // tm
    slab, rem = divmod(rows, nb * mb)
    assert rem == 0 and slab % BF16_SUBLANES == 0
    return pl.pallas_call(
        functools.partial(_in_proj_rope_kernel, n_q_tiles=q_cols // tn),
        grid=(nb, mb),
        in_specs=[pl.BlockSpec((tm, k), lambda j, i: (i, 0)),
                  pl.BlockSpec((k, tn), lambda j, i: (0, j + jb)),
                  pl.BlockSpec((tm, HEAD_WIDTH), lambda j, i: (i % sb, 0)),
                  pl.BlockSpec((tm, HEAD_WIDTH), lambda j, i: (i % sb, 0)),
                  pl.BlockSpec((slab, d), lambda j, i: (j * mb + i, 0))],
        out_specs=[pl.BlockSpec((tm, tn), lambda j, i: (i, j)),
                   pl.BlockSpec((slab, d), lambda j, i: (j * mb + i, 0))],
        out_shape=[jax.ShapeDtypeStruct((m, ncols), jnp.bfloat16),
                   jax.ShapeDtypeStruct((rows, d), jnp.bfloat16)],
        scratch_shapes=[pltpu.VMEM((k, tn), jnp.bfloat16)],
        compiler_params=_params(("arbitrary", "arbitrary")),
        name="in_proj_qk",
    )(a, w, cos_t, sin_t, src)


def _in_proj_convert(a, w, src, *, col0, ncols, tm=1024, tn=1024):
    m, k = a.shape
    rows, d = src.shape
    assert col0 % tn == 0
    jb = col0 // tn
    nb, mb = ncols // tn, m // tm
    slab, rem = divmod(rows, nb * mb)
    assert rem == 0 and slab % BF16_SUBLANES == 0
    return pl.pallas_call(
        _in_proj_kernel,
        grid=(nb, mb),
        in_specs=[pl.BlockSpec((tm, k), lambda j, i: (i, 0)),
                  pl.BlockSpec((k, tn), lambda j, i: (0, j + jb)),
                  pl.BlockSpec((slab, d), lambda j, i: (j * mb + i, 0))],
        out_specs=[pl.BlockSpec((tm, tn), lambda j, i: (i, j)),
                   pl.BlockSpec((slab, d), lambda j, i: (j * mb + i, 0))],
        out_shape=[jax.ShapeDtypeStruct((m, ncols), jnp.bfloat16),
                   jax.ShapeDtypeStruct((rows, d), jnp.bfloat16)],
        scratch_shapes=[pltpu.VMEM((k, tn), jnp.bfloat16)],
        compiler_params=_params(("arbitrary", "arbitrary")),
        name="in_proj_v",
    )(a, w, src)


def _attn_kernel(q_ref, k_ref, v_ref, lq1_ref, lk1_ref, lq2_ref, lk2_ref, gain_ref,
                 o_ref, vt_ref, *st_refs, tq):
    seq = q_ref.shape[1]
    n_tiles = seq // tq
    kc = tq
    hw = HEAD_WIDTH

    vt_ref[:hw, :] = v_ref[0].astype(jnp.float32).T.astype(jnp.bfloat16)
    vt_ref[hw:, :] = jnp.ones((vt_ref.shape[0] - hw, seq), jnp.bfloat16)

    lam = (jnp.exp(jnp.sum(lq1_ref[...] * lk1_ref[...], axis=-1, keepdims=True))
           - jnp.exp(jnp.sum(lq2_ref[...] * lk2_ref[...], axis=-1, keepdims=True))
           + LAMBDA_INIT)

    lane = lax.broadcasted_iota(jnp.int32, (tq, hw), 1)
    key = lax.broadcasted_iota(jnp.int32, (kc, 2 * tq), 0)
    qry = lax.broadcasted_iota(jnp.int32, (kc, 2 * tq), 1) % tq

    def masked_queries(i):
        q = q_ref[0, i * tq:(i + 1) * tq, :]
        return jnp.concatenate([jnp.where(lane < HEAD_DIM, q, 0),
                                jnp.where(lane >= HEAD_DIM, q, 0)], axis=0)

    def scores(i, c, q2, slot):
        rows = slice(c * kc, (c + 1) * kc)
        s = lax.dot_general(k_ref[0, rows, :], q2, (((1,), (1,)), ((), ())),
                            preferred_element_type=jnp.float32)
        if c == i:
            s = jnp.where((qry // CHUNK) >= (key // CHUNK), s, -jnp.inf)
        st_refs[slot][...] = s
        return jnp.max(s, axis=0, keepdims=True)

    def values(c, m, slot):
        rows = slice(c * kc, (c + 1) * kc)
        p = jnp.exp2(st_refs[slot][...] - m).astype(jnp.bfloat16)
        return jnp.dot(vt_ref[:, rows], p, preferred_element_type=jnp.float32)

    items = [(i, c) for i in range(n_tiles - 1, -1, -1) for c in range(i + 1)]
    lookahead = len(st_refs) - 1
    q2s, refs = {}, {}
    m_run = [None]

    def emit_scores(n):
        i, c = items[n]
        if i not in q2s:
            q2s[i] = masked_queries(i)
        m_c = scores(i, c, q2s[i], n % len(st_refs))
        if c == 0:
            refs[n] = (m_c, None)
        else:
            m_new = jnp.maximum(m_run[0], m_c)
            refs[n] = (m_new, jnp.exp2(m_run[0] - m_new))
        m_run[0] = refs[n][0]

    for n in range(min(lookahead, len(items))):
        emit_scores(n)
    acc = None
    for n, (i, c) in enumerate(items):
        if n + lookahead < len(items):
            emit_scores(n + lookahead)
        m_new, alpha = refs.pop(n)
        part = values(c, m_new, n % len(st_refs))
        acc = part if c == 0 else acc * alpha + part
        if c < i:
            continue
        ot = acc[:hw] * (1.0 / acc[hw:hw + 1])
        o = (ot[:, :tq] - lam * ot[:, tq:]).T
        o = _rms_scale(o) * gain_ref[...] * (1.0 - LAMBDA_INIT)
        o_ref[0, i * tq:(i + 1) * tq, :] = o.astype(o_ref.dtype)


def _attention(qk3, v3, lq1, lk1, lq2, lk2, gain, n_heads, tq=256, n_slots=4):
    b, s, _ = v3.shape
    hw = HEAD_WIDTH
    vec = lambda a: a.reshape(1, -1)
    small = lambda n: pl.BlockSpec((1, n), lambda bi, h: (0, 0))
    return pl.pallas_call(
        functools.partial(_attn_kernel, tq=tq),
        grid=(b, n_heads),
        in_specs=[pl.BlockSpec((1, s, hw), lambda bi, h: (bi, 0, h)),
                  pl.BlockSpec((1, s, hw), lambda bi, h: (bi, 0, n_heads + h)),
                  pl.BlockSpec((1, s, hw), lambda bi, h: (bi, 0, h)),
                  small(HEAD_DIM), small(HEAD_DIM), small(HEAD_DIM), small(HEAD_DIM),
                  small(hw)],
        out_specs=pl.BlockSpec((1, s, hw), lambda bi, h: (bi, 0, h)),
        out_shape=jax.ShapeDtypeStruct((b, s, n_heads * hw), jnp.bfloat16),
        scratch_shapes=[pltpu.VMEM((hw + BF16_SUBLANES, s), jnp.bfloat16)]
        + [pltpu.VMEM((tq, 2 * tq), jnp.float32)] * n_slots,
        compiler_params=_params(("arbitrary", "arbitrary")),
        name="diff_attn",
    )(qk3, qk3, v3, vec(lq1), vec(lk1), vec(lq2), vec(lk2), vec(gain))


def _out_proj_kernel(p_ref, a_ref, x_ref, w_ref, g_ref, x1_ref, h_ref):
    kp = p_ref.shape[1]
    for rows in _row_subtiles(x_ref.shape[0]):
        y = jnp.dot(p_ref[rows, :], w_ref[:kp, :], preferred_element_type=jnp.float32)
        y += jnp.dot(a_ref[rows, :], w_ref[kp:, :], preferred_element_type=jnp.float32)
        x1_ref[rows, :] = x_ref[rows, :] + y
        h_ref[rows, :] = (_rms_scale(x1_ref[rows, :]) * g_ref[...]).astype(h_ref.dtype)


def _out_proj(pool_out, attn_out, x2d, w_out_bf16, g, tm=512):
    m, d = x2d.shape
    kp, ka = pool_out.shape[1], attn_out.shape[1]
    return pl.pallas_call(
        _out_proj_kernel,
        grid=(m // tm,),
        in_specs=[pl.BlockSpec((tm, kp), lambda i: (i, 0)),
                  pl.BlockSpec((tm, ka), lambda i: (i, 0)),
                  pl.BlockSpec((tm, d), lambda i: (i, 0)),
                  pl.BlockSpec((kp + ka, d), lambda i: (0, 0),
                               pipeline_mode=pl.Buffered(1)),
                  pl.BlockSpec((1, d), lambda i: (0, 0))],
        out_specs=[pl.BlockSpec((tm, d), lambda i: (i, 0)),
                   pl.BlockSpec((tm, d), lambda i: (i, 0))],
        out_shape=[jax.ShapeDtypeStruct((m, d), jnp.float32),
                   jax.ShapeDtypeStruct((m, d), jnp.bfloat16)],
        compiler_params=_params(("arbitrary",)),
        name="out_proj",
    )(pool_out, attn_out, x2d, w_out_bf16, g.reshape(1, d))


def _gate_up_kernel(a_ref, wg_ref, wu_ref, o_ref, wgb_ref, wub_ref):
    @pl.when(pl.program_id(1) == 0)
    def _():
        wgb_ref[...] = wg_ref[...].astype(jnp.bfloat16)
        wub_ref[...] = wu_ref[...].astype(jnp.bfloat16)

    for rows in _row_subtiles(a_ref.shape[0]):
        a = a_ref[rows, :]
        gate = jnp.dot(a, wgb_ref[...], preferred_element_type=jnp.float32)
        up = jnp.dot(a, wub_ref[...], preferred_element_type=jnp.float32)
        o_ref[rows, :] = (gate * jax.nn.sigmoid(gate) * up).astype(o_ref.dtype)


def _gate_up(a, w_gate_up, hidden, tm=2048, th=512):
    m, k = a.shape
    nb = hidden // th
    return pl.pallas_call(
        _gate_up_kernel,
        grid=(nb, m // tm),
        in_specs=[pl.BlockSpec((tm, k), lambda j, i: (i, 0)),
                  pl.BlockSpec((k, th), lambda j, i: (0, j)),
                  pl.BlockSpec((k, th), lambda j, i: (0, j + nb))],
        out_specs=pl.BlockSpec((tm, th), lambda j, i: (i, j)),
        out_shape=jax.ShapeDtypeStruct((m, hidden), jnp.bfloat16),
        scratch_shapes=[pltpu.VMEM((k, th), jnp.bfloat16),
                        pltpu.VMEM((k, th), jnp.bfloat16)],
        compiler_params=_params(("arbitrary", "arbitrary")),
        name="gate_up",
    )(a, w_gate_up, w_gate_up)


def _down_kernel(a_ref, w_ref, x1_ref, g_ref, o_ref):
    for rows in _row_subtiles(o_ref.shape[0]):
        o_ref[rows, :] = x1_ref[rows, :] + jnp.dot(a_ref[rows, :], w_ref[...],
                                                   preferred_element_type=jnp.float32)
        o_ref[rows, :] = _rms_scale(o_ref[rows, :]) * g_ref[...]


def _down(a, w_down_bf16, x1, g, tm=512):
    m, hidden = a.shape
    d = w_down_bf16.shape[1]
    return pl.pallas_call(
        _down_kernel,
        grid=(m // tm,),
        in_specs=[pl.BlockSpec((tm, hidden), lambda i: (i, 0)),
                  pl.BlockSpec((hidden, d), lambda i: (0, 0), pipeline_mode=pl.Buffered(1)),
                  pl.BlockSpec((tm, d), lambda i: (i, 0)),
                  pl.BlockSpec((1, d), lambda i: (0, 0))],
        out_specs=pl.BlockSpec((tm, d), lambda i: (i, 0)),
        out_shape=jax.ShapeDtypeStruct((m, d), jnp.float32),
        compiler_params=_params(("arbitrary",)),
        name="down",
    )(a, w_down_bf16, x1, g.reshape(1, d))


def _rotary_tables(seq):
    inv_freq = ROPE_THETA ** (-jnp.arange(0, HEAD_DIM, 2, dtype=jnp.float32) / HEAD_DIM)
    ang = jnp.arange(seq, dtype=jnp.float32)[:, None] * inv_freq[None, :]
    cos, sin = jnp.cos(ang), jnp.sin(ang)
    reps = HEAD_WIDTH // HEAD_DIM
    cos_t = jnp.tile(jnp.concatenate([cos, cos], axis=-1), (1, reps))
    sin_t = jnp.tile(jnp.concatenate([-sin, sin], axis=-1), (1, reps))
    return cos_t, sin_t


def kernel(x, norm_mix, w_in, w_pool, pool_scale, lambda_q1, lambda_k1, lambda_q2,
           lambda_k2, subln_gain, w_out, norm_ffn, w_gate_up, w_down, norm_final):
    b, s, d = x.shape
    depth = w_in.shape[0]
    assert depth == 1, "single-layer block"
    pool_width = w_pool.shape[1] * w_pool.shape[2]
    attn_width = w_out.shape[1] - pool_width
    n_heads = attn_width // HEAD_WIDTH

    cos_t, sin_t = _rotary_tables(s)
    x2d = x.reshape(b * s, d)
    l = 0
    h1, pool_out = _norm_pool(x2d, norm_mix[l], w_in[l], w_pool[l], pool_scale[l], s)
    qk_width = n_heads * HEAD_WIDTH
    qk, w_down_bf16 = _in_proj_rope(h1, w_in[l], cos_t, sin_t, w_down[l], col0=pool_width,
                                    ncols=2 * qk_width, q_cols=qk_width)
    v, w_out_bf16 = _in_proj_convert(h1, w_in[l], w_out[l], col0=pool_width + 2 * qk_width,
                                     ncols=attn_width)
    attn_out = _attention(qk.reshape(b, s, -1), v.reshape(b, s, -1), lambda_q1[l],
                          lambda_k1[l], lambda_q2[l], lambda_k2[l], subln_gain[l], n_heads)
    x1, h2 = _out_proj(pool_out, attn_out.reshape(b * s, -1), x2d,
                       w_out_bf16, norm_ffn[l])
    a = _gate_up(h2, w_gate_up[l], w_down.shape[1])
    out = _down(a, w_down_bf16, x1, norm_final)
    return out.reshape(b, s, d)
```

```python
import functools
import math

import jax
import jax.numpy as jnp
from jax import lax
from jax.experimental import pallas as pl
from jax.experimental.pallas import tpu as pltpu

CHUNK = 64
POOL_WINDOWS = (2, 4, 8, 16)
HEAD_DIM = 64
HEAD_WIDTH = 2 * HEAD_DIM
ROPE_THETA = 10000.0
NORM_EPS = 1e-6
LAMBDA_INIT = 0.8 - 0.6 * math.exp(-0.3 * 0)

V7X_VMEM_LIMIT_BYTES = 56 * 1024 * 1024

F32_SUBLANES = 8
BF16_SUBLANES = 16

LOG2_E = math.log2(math.e)


def _params(semantics):
    return pltpu.CompilerParams(dimension_semantics=semantics,
                                vmem_limit_bytes=V7X_VMEM_LIMIT_BYTES)


def _row_subtiles(rows, n=2):
    step, rem = divmod(rows, n)
    assert rem == 0
    return [slice(r * step, (r + 1) * step) for r in range(n)]


def _rms_scale(x):
    return x * lax.rsqrt(jnp.mean(x * x, axis=-1, keepdims=True) + NORM_EPS)


POOL_HALO = 16


def _shift_rows_down(s, shift):
    if shift == F32_SUBLANES:
        return jnp.concatenate([s[:shift], s[:-shift]], axis=0)
    return pltpu.roll(s, shift, axis=0)


def _pool_group(prev, u, window, is_seq_start):
    s = jnp.concatenate([prev, u], axis=0)
    shift = 1
    while shift < window:
        s = s + _shift_rows_down(s, shift)
        shift *= 2
    s = s[POOL_HALO:]
    pooled = s * (1.0 / window) - u
    cnt = (lax.broadcasted_iota(jnp.int32, (POOL_HALO, u.shape[1]), 0) + 1).astype(jnp.float32)
    cnt = jnp.where(is_seq_start, jnp.minimum(cnt, float(window)), float(window))
    top = s[:POOL_HALO] / cnt - u[:POOL_HALO]
    return jnp.concatenate([top, pooled[POOL_HALO:]], axis=0)


def _norm_pool_kernel(x_ref, g_ref, w_ref, wp_ref, ps_ref, h_ref, o_ref,
                      wbf_ref, wpb_ref, halo_ref, *, tiles_per_seq):
    i = pl.program_id(0)

    @pl.when(i == 0)
    def _():
        wbf_ref[...] = w_ref[...].astype(jnp.bfloat16)
        wpb_ref[...] = wp_ref[...].astype(jnp.bfloat16)
        halo_ref[...] = jnp.zeros_like(halo_ref)

    parts = []
    for rows in _row_subtiles(x_ref.shape[0]):
        h = (_rms_scale(x_ref[rows, :]) * g_ref[...]).astype(jnp.bfloat16)
        h_ref[rows, :] = h
        parts.append(jnp.dot(h, wbf_ref[...], preferred_element_type=jnp.float32))
    u = jnp.concatenate(parts, axis=0)

    is_seq_start = (i % tiles_per_seq) == 0
    prev = jnp.where(is_seq_start, 0.0, halo_ref[...])
    halo_ref[...] = u[-POOL_HALO:]
    cg = wp_ref.shape[1]
    for gi, window in enumerate(POOL_WINDOWS):
        cols = slice(gi * cg, (gi + 1) * cg)
        pooled = _pool_group(prev[:, cols], u[:, cols], window, is_seq_start)
        mapped = jnp.dot(pooled.astype(jnp.bfloat16), wpb_ref[gi],
                         preferred_element_type=jnp.float32)
        o_ref[:, cols] = (mapped * ps_ref[:, cols]).astype(o_ref.dtype)


def _norm_pool(x2d, g, w, w_pool, pool_scale, seq, tm=512):
    m, d = x2d.shape
    ng, cg, _ = w_pool.shape
    pw = ng * cg
    assert seq % tm == 0 and max(POOL_WINDOWS) <= POOL_HALO
    return pl.pallas_call(
        functools.partial(_norm_pool_kernel, tiles_per_seq=seq // tm),
        grid=(m // tm,),
        in_specs=[pl.BlockSpec((tm, d), lambda i: (i, 0)),
                  pl.BlockSpec((1, d), lambda i: (0, 0)),
                  pl.BlockSpec((d, pw), lambda i: (0, 0), pipeline_mode=pl.Buffered(1)),
                  pl.BlockSpec((ng, cg, cg), lambda i: (0, 0, 0), pipeline_mode=pl.Buffered(1)),
                  pl.BlockSpec((1, pw), lambda i: (0, 0))],
        out_specs=[pl.BlockSpec((tm, d), lambda i: (i, 0)),
                   pl.BlockSpec((tm, pw), lambda i: (i, 0))],
        out_shape=[jax.ShapeDtypeStruct((m, d), jnp.bfloat16),
                   jax.ShapeDtypeStruct((m, pw), jnp.bfloat16)],
        scratch_shapes=[pltpu.VMEM((d, pw), jnp.bfloat16),
                        pltpu.VMEM((ng, cg, cg), jnp.bfloat16),
                        pltpu.VMEM((POOL_HALO, pw), jnp.float32)],
        compiler_params=_params(("arbitrary",)),
        name="norm_pool",
    )(x2d, g.reshape(1, d), w, w_pool, pool_scale.reshape(1, pw))


def _rope(t, cos_t, sin_t):
    width = t.shape[-1]
    half = HEAD_DIM // 2
    lane = lax.broadcasted_iota(jnp.int32, t.shape, 1)
    first_half = (lane % HEAD_DIM) < half
    partner = jnp.where(first_half,
                        pltpu.roll(t, width - half, axis=1),
                        pltpu.roll(t, half, axis=1))
    return t * cos_t + partner * sin_t


def _proj_attn_kernel(h_ref, wq_ref, wk_ref, wv_ref, cos_ref, sin_ref, lq1_ref, lk1_ref,
                      lq2_ref, lk2_ref, gain_ref, wo_src_ref, wd_src_ref,
                      o_ref, wo_dst_ref, wd_dst_ref,
                      wbf_ref, q_ref, k_ref, vt_ref, *st_refs, tq, rb):
    seq = h_ref.shape[1]
    hw = HEAD_WIDTH
    gw = q_ref.shape[1]
    heads = gw // hw
    n_tiles = seq // tq
    kc = tq
    n_row_blocks = seq // rb

    wo_dst_ref[...] = wo_src_ref[...].astype(jnp.bfloat16)
    wd_dst_ref[...] = wd_src_ref[...].astype(jnp.bfloat16)

    wbf_ref[:, :gw] = wq_ref[...].astype(jnp.bfloat16)
    wbf_ref[:, gw:2 * gw] = wk_ref[...].astype(jnp.bfloat16)
    wbf_ref[:, 2 * gw:] = wv_ref[...].astype(jnp.bfloat16)

    for hd in range(heads):
        vt_ref[hd, hw:, :] = jnp.ones((vt_ref.shape[1] - hw, seq), jnp.bfloat16)

    lam = (jnp.exp(jnp.sum(lq1_ref[...] * lk1_ref[...], axis=-1, keepdims=True))
           - jnp.exp(jnp.sum(lq2_ref[...] * lk2_ref[...], axis=-1, keepdims=True))
           + LAMBDA_INIT)

    lane = lax.broadcasted_iota(jnp.int32, (tq, hw), 1)
    key = lax.broadcasted_iota(jnp.int32, (kc, 2 * tq), 0)
    qry = lax.broadcasted_iota(jnp.int32, (kc, 2 * tq), 1) % tq

    def project(r):
        rows = slice(r * rb, (r + 1) * rb)
        y = jnp.dot(h_ref[0, rows, :], wbf_ref[...], preferred_element_type=jnp.float32)
        cos_t, sin_t = cos_ref[rows, :], sin_ref[rows, :]
        q_scale = LOG2_E / math.sqrt(HEAD_DIM)
        cos_q, sin_q = cos_t * q_scale, sin_t * q_scale
        for hd in range(heads):
            cols = slice(hd * hw, (hd + 1) * hw)
            q_ref[rows, cols] = _rope(y[:, cols], cos_q, sin_q).astype(jnp.bfloat16)
            k_ref[rows, cols] = _rope(y[:, gw + hd * hw:gw + (hd + 1) * hw],
                                      cos_t, sin_t).astype(jnp.bfloat16)
            vt_ref[hd, :hw, rows] = y[:, 2 * gw + hd * hw:2 * gw + (hd + 1) * hw
                                      ].T.astype(jnp.bfloat16)

    projected = []

    def ensure_projected(r):
        while len(projected) <= min(r, n_row_blocks - 1):
            project(len(projected))
            projected.append(True)

    def masked_queries(hd, i):
        q = q_ref[i * tq:(i + 1) * tq, hd * hw:(hd + 1) * hw]
        return jnp.concatenate([jnp.where(lane < HEAD_DIM, q, 0),
                                jnp.where(lane >= HEAD_DIM, q, 0)], axis=0)

    def scores(hd, i, c, q2, slot):
        s = lax.dot_general(k_ref[c * kc:(c + 1) * kc, hd * hw:(hd + 1) * hw], q2,
                            (((1,), (1,)), ((), ())),
                            preferred_element_type=jnp.float32)
        if c == i:
            s = jnp.where((qry // CHUNK) >= (key // CHUNK), s, -jnp.inf)
        st_refs[slot][...] = s
        return jnp.max(s, axis=0, keepdims=True)

    def values(hd, c, m, slot):
        p = jnp.exp2(st_refs[slot][...] - m).astype(jnp.bfloat16)
        return jnp.dot(vt_ref[hd, :, c * kc:(c + 1) * kc], p,
                       preferred_element_type=jnp.float32)

    items = [(hd, i, c) for i in range(n_tiles) for hd in range(heads) for c in range(i + 1)]
    lookahead = len(st_refs) - 1
    q2s, refs = {}, {}
    m_run = [None]

    def block_of(i):
        return ((i + 1) * tq - 1) // rb

    def emit_scores(n):
        hd, i, c = items[n]
        ensure_projected(block_of(i))
        if (hd, i) not in q2s:
            q2s[hd, i] = masked_queries(hd, i)
        m_c = scores(hd, i, c, q2s[hd, i], n % len(st_refs))
        if c == 0:
            refs[n] = (m_c, None)
        else:
            m_new = jnp.maximum(m_run[0], m_c)
            refs[n] = (m_new, jnp.exp2(m_run[0] - m_new))
        m_run[0] = refs[n][0]

    ensure_projected(1)
    for n in range(min(lookahead, len(items))):
        emit_scores(n)
    acc = None
    for n, (hd, i, c) in enumerate(items):
        ensure_projected(block_of(i) + 1)
        if n + lookahead < len(items):
            emit_scores(n + lookahead)
        m_new, alpha = refs.pop(n)
        part = values(hd, c, m_new, n % len(st_refs))
        acc = part if c == 0 else acc * alpha + part
        if c < i:
            continue
        ot = acc[:hw] * (1.0 / acc[hw:hw + 1])
        o = (ot[:, :tq] - lam * ot[:, tq:]).T
        o = _rms_scale(o) * gain_ref[...] * (1.0 - LAMBDA_INIT)
        o_ref[0, i * tq:(i + 1) * tq, hd * hw:(hd + 1) * hw] = o.astype(o_ref.dtype)


def _proj_attention(h3, w_in, cos_t, sin_t, lq1, lk1, lq2, lk2, gain, w_out, w_down, *,
                    q_col0, n_heads, heads_per_step=2, tq=256, rb=512, n_slots=4):
    b, s, d = h3.shape
    hw = HEAD_WIDTH
    gw = heads_per_step * hw
    width = n_heads * hw
    nj = n_heads // heads_per_step
    assert q_col0 % gw == 0 and width % gw == 0 and s % rb == 0 and rb % tq == 0
    qb, kb, vb = q_col0 // gw, (q_col0 + width) // gw, (q_col0 + 2 * width) // gw
    steps = b * nj
    wo_slab, wo_rem = divmod(w_out.shape[0], steps)
    wd_slab, wd_rem = divmod(w_down.shape[0], steps)
    assert wo_rem == 0 and wd_rem == 0
    assert wo_slab % BF16_SUBLANES == 0 and wd_slab % BF16_SUBLANES == 0
    vec = lambda a: a.reshape(1, -1)
    small = lambda n: pl.BlockSpec((1, n), lambda bi, j: (0, 0))
    table = pl.BlockSpec((s, hw), lambda bi, j: (0, 0), pipeline_mode=pl.Buffered(1))
    slab = lambda rows, cols: pl.BlockSpec((rows, cols), lambda bi, j: (bi * nj + j, 0))
    return pl.pallas_call(
        functools.partial(_proj_attn_kernel, tq=tq, rb=rb),
        grid=(b, nj),
        in_specs=[pl.BlockSpec((1, s, d), lambda bi, j: (bi, 0, 0)),
                  pl.BlockSpec((d, gw), lambda bi, j: (0, qb + j)),
                  pl.BlockSpec((d, gw), lambda bi, j: (0, kb + j)),
                  pl.BlockSpec((d, gw), lambda bi, j: (0, vb + j)),
                  table, table,
                  small(HEAD_DIM), small(HEAD_DIM), small(HEAD_DIM), small(HEAD_DIM),
                  small(hw),
                  slab(wo_slab, w_out.shape[1]), slab(wd_slab, w_down.shape[1])],
        out_specs=[pl.BlockSpec((1, s, gw), lambda bi, j: (bi, 0, j)),
                   slab(wo_slab, w_out.shape[1]), slab(wd_slab, w_down.shape[1])],
        out_shape=[jax.ShapeDtypeStruct((b, s, width), jnp.bfloat16),
                   jax.ShapeDtypeStruct(w_out.shape, jnp.bfloat16),
                   jax.ShapeDtypeStruct(w_down.shape, jnp.bfloat16)],
        scratch_shapes=[pltpu.VMEM((d, 3 * gw), jnp.bfloat16),
                        pltpu.VMEM((s, gw), jnp.bfloat16),
                        pltpu.VMEM((s, gw), jnp.bfloat16),
                        pltpu.VMEM((heads_per_step, hw + BF16_SUBLANES, s), jnp.bfloat16)]
        + [pltpu.VMEM((tq, 2 * tq), jnp.float32)] * n_slots,
        compiler_params=_params(("arbitrary", "arbitrary")),
        name="proj_attn",
    )(h3, w_in, w_in, w_in, cos_t, sin_t, vec(lq1), vec(lk1), vec(lq2), vec(lk2), vec(gain),
      w_out, w_down)


def _out_proj_kernel(p_ref, a_ref, x_ref, w_ref, g_ref, x1_ref, h_ref):
    kp = p_ref.shape[1]
    for rows in _row_subtiles(x_ref.shape[0]):
        y = jnp.dot(p_ref[rows, :], w_ref[:kp, :], preferred_element_type=jnp.float32)
        y += jnp.dot(a_ref[rows, :], w_ref[kp:, :], preferred_element_type=jnp.float32)
        x1_ref[rows, :] = x_ref[rows, :] + y
        h_ref[rows, :] = (_rms_scale(x1_ref[rows, :]) * g_ref[...]).astype(h_ref.dtype)


def _out_proj(pool_out, attn_out, x2d, w_out_bf16, g, tm=512):
    m, d = x2d.shape
    kp, ka = pool_out.shape[1], attn_out.shape[1]
    return pl.pallas_call(
        _out_proj_kernel,
        grid=(m // tm,),
        in_specs=[pl.BlockSpec((tm, kp), lambda i: (i, 0)),
                  pl.BlockSpec((tm, ka), lambda i: (i, 0)),
                  pl.BlockSpec((tm, d), lambda i: (i, 0)),
                  pl.BlockSpec((kp + ka, d), lambda i: (0, 0),
                               pipeline_mode=pl.Buffered(1)),
                  pl.BlockSpec((1, d), lambda i: (0, 0))],
        out_specs=[pl.BlockSpec((tm, d), lambda i: (i, 0)),
                   pl.BlockSpec((tm, d), lambda i: (i, 0))],
        out_shape=[jax.ShapeDtypeStruct((m, d), jnp.float32),
                   jax.ShapeDtypeStruct((m, d), jnp.bfloat16)],
        compiler_params=_params(("arbitrary",)),
        name="out_proj",
    )(pool_out, attn_out, x2d, w_out_bf16, g.reshape(1, d))


def _gate_up_kernel(a_ref, wg_ref, wu_ref, o_ref, wgb_ref, wub_ref):
    @pl.when(pl.program_id(1) == 0)
    def _():
        wgb_ref[...] = wg_ref[...].astype(jnp.bfloat16)
        wub_ref[...] = wu_ref[...].astype(jnp.bfloat16)

    for rows in _row_subtiles(a_ref.shape[0]):
        a = a_ref[rows, :]
        gate = jnp.dot(a, wgb_ref[...], preferred_element_type=jnp.float32)
        up = jnp.dot(a, wub_ref[...], preferred_element_type=jnp.float32)
        o_ref[rows, :] = (gate * jax.nn.sigmoid(gate) * up).astype(o_ref.dtype)


def _gate_up(a, w_gate_up, hidden, tm=2048, th=512):
    m, k = a.shape
    nb = hidden // th
    return pl.pallas_call(
        _gate_up_kernel,
        grid=(nb, m // tm),
        in_specs=[pl.BlockSpec((tm, k), lambda j, i: (i, 0)),
                  pl.BlockSpec((k, th), lambda j, i: (0, j)),
                  pl.BlockSpec((k, th), lambda j, i: (0, j + nb))],
        out_specs=pl.BlockSpec((tm, th), lambda j, i: (i, j)),
        out_shape=jax.ShapeDtypeStruct((m, hidden), jnp.bfloat16),
        scratch_shapes=[pltpu.VMEM((k, th), jnp.bfloat16),
                        pltpu.VMEM((k, th), jnp.bfloat16)],
        compiler_params=_params(("arbitrary", "arbitrary")),
        name="gate_up",
    )(a, w_gate_up, w_gate_up)


def _down_kernel(a_ref, w_ref, x1_ref, g_ref, o_ref):
    for rows in _row_subtiles(o_ref.shape[0]):
        o_ref[rows, :] = x1_ref[rows, :] + jnp.dot(a_ref[rows, :], w_ref[...],
                                                   preferred_element_type=jnp.float32)
        o_ref[rows, :] = _rms_scale(o_ref[rows, :]) * g_ref[...]


def _down(a, w_down_bf16, x1, g, tm=512):
    m, hidden = a.shape
    d = w_down_bf16.shape[1]
    return pl.pallas_call(
        _down_kernel,
        grid=(m // tm,),
        in_specs=[pl.BlockSpec((tm, hidden), lambda i: (i, 0)),
                  pl.BlockSpec((hidden, d), lambda i: (0, 0), pipeline_mode=pl.Buffered(1)),
                  pl.BlockSpec((tm, d), lambda i: (i, 0)),
                  pl.BlockSpec((1, d), lambda i: (0, 0))],
        out_specs=pl.BlockSpec((tm, d), lambda i: (i, 0)),
        out_shape=jax.ShapeDtypeStruct((m, d), jnp.float32),
        compiler_params=_params(("arbitrary",)),
        name="down",
    )(a, w_down_bf16, x1, g.reshape(1, d))


def _rotary_tables(seq):
    inv_freq = ROPE_THETA ** (-jnp.arange(0, HEAD_DIM, 2, dtype=jnp.float32) / HEAD_DIM)
    ang = jnp.arange(seq, dtype=jnp.float32)[:, None] * inv_freq[None, :]
    cos, sin = jnp.cos(ang), jnp.sin(ang)
    reps = HEAD_WIDTH // HEAD_DIM
    cos_t = jnp.tile(jnp.concatenate([cos, cos], axis=-1), (1, reps))
    sin_t = jnp.tile(jnp.concatenate([-sin, sin], axis=-1), (1, reps))
    return cos_t, sin_t


def kernel(x, norm_mix, w_in, w_pool, pool_scale, lambda_q1, lambda_k1, lambda_q2,
           lambda_k2, subln_gain, w_out, norm_ffn, w_gate_up, w_down, norm_final):
    b, s, d = x.shape
    depth = w_in.shape[0]
    assert depth == 1, "single-layer block"
    pool_width = w_pool.shape[1] * w_pool.shape[2]
    attn_width = w_out.shape[1] - pool_width
    n_heads = attn_width // HEAD_WIDTH

    cos_t, sin_t = _rotary_tables(s)
    x2d = x.reshape(b * s, d)
    l = 0
    h1, pool_out = _norm_pool(x2d, norm_mix[l], w_in[l], w_pool[l], pool_scale[l], s)
    attn_out, w_out_bf16, w_down_bf16 = _proj_attention(
        h1.reshape(b, s, d), w_in[l], cos_t, sin_t, lambda_q1[l], lambda_k1[l], lambda_q2[l],
        lambda_k2[l], subln_gain[l], w_out[l], w_down[l], q_col0=pool_width, n_heads=n_heads)
    x1, h2 = _out_proj(pool_out, attn_out.reshape(b * s, -1), x2d,
                       w_out_bf16, norm_ffn[l])
    a = _gate_up(h2, w_gate_up[l], w_down.shape[1])
    out = _down(a, w_down_bf16, x1, norm_final)
    return out.reshape(b, s, d)
```

```python
import functools
import math

import jax
import jax.numpy as jnp
from jax import lax
from jax.experimental import pallas as pl
from jax.experimental.pallas import tpu as pltpu

CHUNK = 64
POOL_WINDOWS = (2, 4, 8, 16)
HEAD_DIM = 64
HEAD_WIDTH = 2 * HEAD_DIM
ROPE_THETA = 10000.0
NORM_EPS = 1e-6
LAMBDA_INIT = 0.8 - 0.6 * math.exp(-0.3 * 0)

V7X_VMEM_LIMIT_BYTES = 56 * 1024 * 1024

F32_SUBLANES = 8
BF16_SUBLANES = 16

LOG2_E = math.log2(math.e)


def _params(semantics):
    return pltpu.CompilerParams(dimension_semantics=semantics,
                                vmem_limit_bytes=V7X_VMEM_LIMIT_BYTES)


def _row_subtiles(rows, n=2):
    step, rem = divmod(rows, n)
    assert rem == 0
    return [slice(r * step, (r + 1) * step) for r in range(n)]


def _rms_scale(x):
    return x * lax.rsqrt(jnp.mean(x * x, axis=-1, keepdims=True) + NORM_EPS)


POOL_HALO = 16


def _shift_rows_down(s, shift):
    if shift == F32_SUBLANES:
        return jnp.concatenate([s[:shift], s[:-shift]], axis=0)
    return pltpu.roll(s, shift, axis=0)


def _pool_group(prev, u, window, is_seq_start):
    s = jnp.concatenate([prev, u], axis=0)
    shift = 1
    while shift < window:
        s = s + _shift_rows_down(s, shift)
        shift *= 2
    s = s[POOL_HALO:]
    pooled = s * (1.0 / window) - u
    cnt = (lax.broadcasted_iota(jnp.int32, (POOL_HALO, u.shape[1]), 0) + 1).astype(jnp.float32)
    cnt = jnp.where(is_seq_start, jnp.minimum(cnt, float(window)), float(window))
    top = s[:POOL_HALO] / cnt - u[:POOL_HALO]
    return jnp.concatenate([top, pooled[POOL_HALO:]], axis=0)


def _norm_pool_kernel(x_ref, g_ref, w_ref, wp_ref, ps_ref, h_ref, o_ref,
                      wbf_ref, wpb_ref, halo_ref, *, tiles_per_seq):
    i = pl.program_id(0)

    @pl.when(i == 0)
    def _():
        wbf_ref[...] = w_ref[...].astype(jnp.bfloat16)
        wpb_ref[...] = wp_ref[...].astype(jnp.bfloat16)
        halo_ref[...] = jnp.zeros_like(halo_ref)

    parts = []
    for rows in _row_subtiles(x_ref.shape[0]):
        h = (_rms_scale(x_ref[rows, :]) * g_ref[...]).astype(jnp.bfloat16)
        h_ref[rows, :] = h
        parts.append(jnp.dot(h, wbf_ref[...], preferred_element_type=jnp.float32))
    u = jnp.concatenate(parts, axis=0)

    is_seq_start = (i % tiles_per_seq) == 0
    prev = jnp.where(is_seq_start, 0.0, halo_ref[...])
    halo_ref[...] = u[-POOL_HALO:]
    cg = wp_ref.shape[1]
    for gi, window in enumerate(POOL_WINDOWS):
        cols = slice(gi * cg, (gi + 1) * cg)
        pooled = _pool_group(prev[:, cols], u[:, cols], window, is_seq_start)
        mapped = jnp.dot(pooled.astype(jnp.bfloat16), wpb_ref[gi],
                         preferred_element_type=jnp.float32)
        o_ref[:, cols] = (mapped * ps_ref[:, cols]).astype(o_ref.dtype)


def _norm_pool(x2d, g, w, w_pool, pool_scale, seq, tm=512):
    m, d = x2d.shape
    ng, cg, _ = w_pool.shape
    pw = ng * cg
    assert seq % tm == 0 and max(POOL_WINDOWS) <= POOL_HALO
    return pl.pallas_call(
        functools.partial(_norm_pool_kernel, tiles_per_seq=seq // tm),
        grid=(m // tm,),
        in_specs=[pl.BlockSpec((tm, d), lambda i: (i, 0)),
                  pl.BlockSpec((1, d), lambda i: (0, 0)),
                  pl.BlockSpec((d, pw), lambda i: (0, 0), pipeline_mode=pl.Buffered(1)),
                  pl.BlockSpec((ng, cg, cg), lambda i: (0, 0, 0), pipeline_mode=pl.Buffered(1)),
                  pl.BlockSpec((1, pw), lambda i: (0, 0))],
        out_specs=[pl.BlockSpec((tm, d), lambda i: (i, 0)),
                   pl.BlockSpec((tm, pw), lambda i: (i, 0))],
        out_shape=[jax.ShapeDtypeStruct((m, d), jnp.bfloat16),
                   jax.ShapeDtypeStruct((m, pw), jnp.bfloat16)],
        scratch_shapes=[pltpu.VMEM((d, pw), jnp.bfloat16),
                        pltpu.VMEM((ng, cg, cg), jnp.bfloat16),
                        pltpu.VMEM((POOL_HALO, pw), jnp.float32)],
        compiler_params=_params(("arbitrary",)),
        name="norm_pool",
    )(x2d, g.reshape(1, d), w, w_pool, pool_scale.reshape(1, pw))


def _rope(t, cos_t, sin_t):
    width = t.shape[-1]
    half = HEAD_DIM // 2
    lane = lax.broadcasted_iota(jnp.int32, t.shape, 1)
    first_half = (lane % HEAD_DIM) < half
    partner = jnp.where(first_half,
                        pltpu.roll(t, width - half, axis=1),
                        pltpu.roll(t, half, axis=1))
    return t * cos_t + partner * sin_t


def _proj_attn_kernel(h_ref, wq_ref, wk_ref, wv_ref, cos_ref, sin_ref, lq1_ref, lk1_ref,
                      lq2_ref, lk2_ref, gain_ref, wo_src_ref, wd_src_ref,
                      o_ref, wo_dst_ref, wd_dst_ref,
                      wbf_ref, q_ref, k_ref, vt_ref, *st_refs, tq, rb):
    seq = h_ref.shape[1]
    hw = HEAD_WIDTH
    gw = q_ref.shape[1]
    heads = gw // hw
    n_tiles = seq // tq
    kc = tq
    n_row_blocks = seq // rb

    wo_dst_ref[...] = wo_src_ref[...].astype(jnp.bfloat16)
    wd_dst_ref[...] = wd_src_ref[...].astype(jnp.bfloat16)

    wbf_ref[:, :gw] = wq_ref[...].astype(jnp.bfloat16)
    wbf_ref[:, gw:2 * gw] = wk_ref[...].astype(jnp.bfloat16)
    wbf_ref[:, 2 * gw:] = wv_ref[...].astype(jnp.bfloat16)

    for hd in range(heads):
        vt_ref[hd, hw:, :] = jnp.ones((vt_ref.shape[1] - hw, seq), jnp.bfloat16)

    lam = (jnp.exp(jnp.sum(lq1_ref[...] * lk1_ref[...], axis=-1, keepdims=True))
           - jnp.exp(jnp.sum(lq2_ref[...] * lk2_ref[...], axis=-1, keepdims=True))
           + LAMBDA_INIT)

    lane = lax.broadcasted_iota(jnp.int32, (tq, hw), 1)
    key = lax.broadcasted_iota(jnp.int32, (kc, 2 * tq), 0)
    qry = lax.broadcasted_iota(jnp.int32, (kc, 2 * tq), 1) % tq

    def project(r):
        rows = slice(r * rb, (r + 1) * rb)
        y = jnp.dot(h_ref[0, rows, :], wbf_ref[...], preferred_element_type=jnp.float32)
        cos_t, sin_t = cos_ref[rows, :], sin_ref[rows, :]
        q_scale = LOG2_E / math.sqrt(HEAD_DIM)
        cos_q, sin_q = cos_t * q_scale, sin_t * q_scale
        for hd in range(heads):
            cols = slice(hd * hw, (hd + 1) * hw)
            q_ref[rows, cols] = _rope(y[:, cols], cos_q, sin_q).astype(jnp.bfloat16)
            k_ref[rows, cols] = _rope(y[:, gw + hd * hw:gw + (hd + 1) * hw],
                                      cos_t, sin_t).astype(jnp.bfloat16)
            vt_ref[hd, :hw, rows] = y[:, 2 * gw + hd * hw:2 * gw + (hd + 1) * hw
                                      ].T.astype(jnp.bfloat16)

    projected = []

    def ensure_projected(r):
        while len(projected) <= min(r, n_row_blocks - 1):
            project(len(projected))
            projected.append(True)

    def masked_queries(hd, i):
        q = q_ref[i * tq:(i + 1) * tq, hd * hw:(hd + 1) * hw]
        return jnp.concatenate([jnp.where(lane < HEAD_DIM, q, 0),
                                jnp.where(lane >= HEAD_DIM, q, 0)], axis=0)

    def scores(hd, i, c, q2, slot):
        s = lax.dot_general(k_ref[c * kc:(c + 1) * kc, hd * hw:(hd + 1) * hw], q2,
                            (((1,), (1,)), ((), ())),
                            preferred_element_type=jnp.float32)
        if c == i:
            s = jnp.where((qry // CHUNK) >= (key // CHUNK), s, -jnp.inf)
        st_refs[slot][...] = s
        return jnp.max(s, axis=0, keepdims=True)

    def values(hd, c, m, slot):
        p = jnp.exp2(st_refs[slot][...] - m).astype(jnp.bfloat16)
        return jnp.dot(vt_ref[hd, :, c * kc:(c + 1) * kc], p,
                       preferred_element_type=jnp.float32)

    items = [(hd, i, c) for i in range(n_tiles) for c in range(i + 1) for hd in range(heads)]
    lookahead = len(st_refs) - 1
    q2s, refs, accs = {}, {}, {}
    m_run = {}

    def block_of(i):
        return ((i + 1) * tq - 1) // rb

    def emit_scores(n):
        hd, i, c = items[n]
        ensure_projected(block_of(i))
        if (hd, i) not in q2s:
            q2s[hd, i] = masked_queries(hd, i)
        m_c = scores(hd, i, c, q2s[hd, i], n % len(st_refs))
        if c == 0:
            refs[n] = (m_c, None)
        else:
            m_new = jnp.maximum(m_run[hd], m_c)
            refs[n] = (m_new, jnp.exp2(m_run[hd] - m_new))
        m_run[hd] = refs[n][0]

    ensure_projected(1)
    for n in range(min(lookahead, len(items))):
        emit_scores(n)
    for n, (hd, i, c) in enumerate(items):
        ensure_projected(block_of(i) + 1)
        if n + lookahead < len(items):
            emit_scores(n + lookahead)
        m_new, alpha = refs.pop(n)
        part = values(hd, c, m_new, n % len(st_refs))
        accs[hd] = part if c == 0 else accs[hd] * alpha + part
        if c < i:
            continue
        acc = accs.pop(hd)
        ot = acc[:hw] * (1.0 / acc[hw:hw + 1])
        o = (ot[:, :tq] - lam * ot[:, tq:]).T
        o = _rms_scale(o) * gain_ref[...] * (1.0 - LAMBDA_INIT)
        o_ref[0, i * tq:(i + 1) * tq, hd * hw:(hd + 1) * hw] = o.astype(o_ref.dtype)


def _proj_attention(h3, w_in, cos_t, sin_t, lq1, lk1, lq2, lk2, gain, w_out, w_down, *,
                    q_col0, n_heads, heads_per_step=2, tq=256, rb=512, n_slots=4):
    b, s, d = h3.shape
    hw = HEAD_WIDTH
    gw = heads_per_step * hw
    width = n_heads * hw
    nj = n_heads // heads_per_step
    assert q_col0 % gw == 0 and width % gw == 0 and s % rb == 0 and rb % tq == 0
    qb, kb, vb = q_col0 // gw, (q_col0 + width) // gw, (q_col0 + 2 * width) // gw
    steps = b * nj
    wo_slab, wo_rem = divmod(w_out.shape[0], steps)
    wd_slab, wd_rem = divmod(w_down.shape[0], steps)
    assert wo_rem == 0 and wd_rem == 0
    assert wo_slab % BF16_SUBLANES == 0 and wd_slab % BF16_SUBLANES == 0
    vec = lambda a: a.reshape(1, -1)
    small = lambda n: pl.BlockSpec((1, n), lambda bi, j: (0, 0))
    table = pl.BlockSpec((s, hw), lambda bi, j: (0, 0), pipeline_mode=pl.Buffered(1))
    slab = lambda rows, cols: pl.BlockSpec((rows, cols), lambda bi, j: (bi * nj + j, 0))
    return pl.pallas_call(
        functools.partial(_proj_attn_kernel, tq=tq, rb=rb),
        grid=(b, nj),
        in_specs=[pl.BlockSpec((1, s, d), lambda bi, j: (bi, 0, 0)),
                  pl.BlockSpec((d, gw), lambda bi, j: (0, qb + j)),
                  pl.BlockSpec((d, gw), lambda bi, j: (0, kb + j)),
                  pl.BlockSpec((d, gw), lambda bi, j: (0, vb + j)),
                  table, table,
                  small(HEAD_DIM), small(HEAD_DIM), small(HEAD_DIM), small(HEAD_DIM),
                  small(hw),
                  slab(wo_slab, w_out.shape[1]), slab(wd_slab, w_down.shape[1])],
        out_specs=[pl.BlockSpec((1, s, gw), lambda bi, j: (bi, 0, j)),
                   slab(wo_slab, w_out.shape[1]), slab(wd_slab, w_down.shape[1])],
        out_shape=[jax.ShapeDtypeStruct((b, s, width), jnp.bfloat16),
                   jax.ShapeDtypeStruct(w_out.shape, jnp.bfloat16),
                   jax.ShapeDtypeStruct(w_down.shape, jnp.bfloat16)],
        scratch_shapes=[pltpu.VMEM((d, 3 * gw), jnp.bfloat16),
                        pltpu.VMEM((s, gw), jnp.bfloat16),
                        pltpu.VMEM((s, gw), jnp.bfloat16),
                        pltpu.VMEM((heads_per_step, hw + BF16_SUBLANES, s), jnp.bfloat16)]
        + [pltpu.VMEM((tq, 2 * tq), jnp.float32)] * n_slots,
        compiler_params=_params(("arbitrary", "arbitrary")),
        name="proj_attn",
    )(h3, w_in, w_in, w_in, cos_t, sin_t, vec(lq1), vec(lk1), vec(lq2), vec(lk2), vec(gain),
      w_out, w_down)


def _out_proj_kernel(p_ref, a_ref, x_ref, w_ref, g_ref, x1_ref, h_ref):
    kp = p_ref.shape[1]
    for rows in _row_subtiles(x_ref.shape[0]):
        y = jnp.dot(p_ref[rows, :], w_ref[:kp, :], preferred_element_type=jnp.float32)
        y += jnp.dot(a_ref[rows, :], w_ref[kp:, :], preferred_element_type=jnp.float32)
        x1_ref[rows, :] = x_ref[rows, :] + y
        h_ref[rows, :] = (_rms_scale(x1_ref[rows, :]) * g_ref[...]).astype(h_ref.dtype)


def _out_proj(pool_out, attn_out, x2d, w_out_bf16, g, tm=512):
    m, d = x2d.shape
    kp, ka = pool_out.shape[1], attn_out.shape[1]
    return pl.pallas_call(
        _out_proj_kernel,
        grid=(m // tm,),
        in_specs=[pl.BlockSpec((tm, kp), lambda i: (i, 0)),
                  pl.BlockSpec((tm, ka), lambda i: (i, 0)),
                  pl.BlockSpec((tm, d), lambda i: (i, 0)),
                  pl.BlockSpec((kp + ka, d), lambda i: (0, 0),
                               pipeline_mode=pl.Buffered(1)),
                  pl.BlockSpec((1, d), lambda i: (0, 0))],
        out_specs=[pl.BlockSpec((tm, d), lambda i: (i, 0)),
                   pl.BlockSpec((tm, d), lambda i: (i, 0))],
        out_shape=[jax.ShapeDtypeStruct((m, d), jnp.float32),
                   jax.ShapeDtypeStruct((m, d), jnp.bfloat16)],
        compiler_params=_params(("arbitrary",)),
        name="out_proj",
    )(pool_out, attn_out, x2d, w_out_bf16, g.reshape(1, d))


def _gate_up_kernel(a_ref, wg_ref, wu_ref, o_ref, wgb_ref, wub_ref):
    @pl.when(pl.program_id(1) == 0)
    def _():
        wgb_ref[...] = wg_ref[...].astype(jnp.bfloat16)
        wub_ref[...] = wu_ref[...].astype(jnp.bfloat16)

    for rows in _row_subtiles(a_ref.shape[0]):
        a = a_ref[rows, :]
        gate = jnp.dot(a, wgb_ref[...], preferred_element_type=jnp.float32)
        up = jnp.dot(a, wub_ref[...], preferred_element_type=jnp.float32)
        o_ref[rows, :] = (gate * jax.nn.sigmoid(gate) * up).astype(o_ref.dtype)


def _gate_up(a, w_gate_up, hidden, tm=2048, th=512):
    m, k = a.shape
    nb = hidden // th
    return pl.pallas_call(
        _gate_up_kernel,
        grid=(nb, m // tm),
        in_specs=[pl.BlockSpec((tm, k), lambda j, i: (i, 0)),
                  pl.BlockSpec((k, th), lambda j, i: (0, j)),
                  pl.BlockSpec((k, th), lambda j, i: (0, j + nb))],
        out_specs=pl.BlockSpec((tm, th), lambda j, i: (i, j)),
        out_shape=jax.ShapeDtypeStruct((m, hidden), jnp.bfloat16),
        scratch_shapes=[pltpu.VMEM((k, th), jnp.bfloat16),
                        pltpu.VMEM((k, th), jnp.bfloat16)],
        compiler_params=_params(("arbitrary", "arbitrary")),
        name="gate_up",
    )(a, w_gate_up, w_gate_up)


def _down_kernel(a_ref, w_ref, x1_ref, g_ref, o_ref):
    for rows in _row_subtiles(o_ref.shape[0]):
        o_ref[rows, :] = x1_ref[rows, :] + jnp.dot(a_ref[rows, :], w_ref[...],
                                                   preferred_element_type=jnp.float32)
        o_ref[rows, :] = _rms_scale(o_ref[rows, :]) * g_ref[...]


def _down(a, w_down_bf16, x1, g, tm=512):
    m, hidden = a.shape
    d = w_down_bf16.shape[1]
    return pl.pallas_call(
        _down_kernel,
        grid=(m // tm,),
        in_specs=[pl.BlockSpec((tm, hidden), lambda i: (i, 0)),
                  pl.BlockSpec((hidden, d), lambda i: (0, 0), pipeline_mode=pl.Buffered(1)),
                  pl.BlockSpec((tm, d), lambda i: (i, 0)),
                  pl.BlockSpec((1, d), lambda i: (0, 0))],
        out_specs=pl.BlockSpec((tm, d), lambda i: (i, 0)),
        out_shape=jax.ShapeDtypeStruct((m, d), jnp.float32),
        compiler_params=_params(("arbitrary",)),
        name="down",
    )(a, w_down_bf16, x1, g.reshape(1, d))


def _rotary_tables(seq):
    inv_freq = ROPE_THETA ** (-jnp.arange(0, HEAD_DIM, 2, dtype=jnp.float32) / HEAD_DIM)
    ang = jnp.arange(seq, dtype=jnp.float32)[:, None] * inv_freq[None, :]
    cos, sin = jnp.cos(ang), jnp.sin(ang)
    reps = HEAD_WIDTH // HEAD_DIM
    cos_t = jnp.tile(jnp.concatenate([cos, cos], axis=-1), (1, reps))
    sin_t = jnp.tile(jnp.concatenate([-sin, sin], axis=-1), (1, reps))
    return cos_t, sin_t


def kernel(x, norm_mix, w_in, w_pool, pool_scale, lambda_q1, lambda_k1, lambda_q2,
           lambda_k2, subln_gain, w_out, norm_ffn, w_gate_up, w_down, norm_final):
    b, s, d = x.shape
    depth = w_in.shape[0]
    assert depth == 1, "single-layer block"
    pool_width = w_pool.shape[1] * w_pool.shape[2]
    attn_width = w_out.shape[1] - pool_width
    n_heads = attn_width // HEAD_WIDTH

    cos_t, sin_t = _rotary_tables(s)
    x2d = x.reshape(b * s, d)
    l = 0
    h1, pool_out = _norm_pool(x2d, norm_mix[l], w_in[l], w_pool[l], pool_scale[l], s)
    attn_out, w_out_bf16, w_down_bf16 = _proj_attention(
        h1.reshape(b, s, d), w_in[l], cos_t, sin_t, lambda_q1[l], lambda_k1[l], lambda_q2[l],
        lambda_k2[l], subln_gain[l], w_out[l], w_down[l], q_col0=pool_width, n_heads=n_heads)
    x1, h2 = _out_proj(pool_out, attn_out.reshape(b * s, -1), x2d,
                       w_out_bf16, norm_ffn[l])
    a = _gate_up(h2, w_gate_up[l], w_down.shape[1])
    out = _down(a, w_down_bf16, x1, norm_final)
    return out.reshape(b, s, d)
```

```python
import functools
import math

import jax
import jax.numpy as jnp
from jax import lax
from jax.experimental import pallas as pl
from jax.experimental.pallas import tpu as pltpu

CHUNK = 64
POOL_WINDOWS = (2, 4, 8, 16)
HEAD_DIM = 64
HEAD_WIDTH = 2 * HEAD_DIM
ROPE_THETA = 10000.0
NORM_EPS = 1e-6
LAMBDA_INIT = 0.8 - 0.6 * math.exp(-0.3 * 0)

V7X_VMEM_LIMIT_BYTES = 56 * 1024 * 1024

F32_SUBLANES = 8
BF16_SUBLANES = 16

LOG2_E = math.log2(math.e)


def _params(semantics):
    return pltpu.CompilerParams(dimension_semantics=semantics,
                                vmem_limit_bytes=V7X_VMEM_LIMIT_BYTES)


def _row_subtiles(rows, n=2):
    step, rem = divmod(rows, n)
    assert rem == 0
    return [slice(r * step, (r + 1) * step) for r in range(n)]


def _rms_scale(x):
    return x * lax.rsqrt(jnp.mean(x * x, axis=-1, keepdims=True) + NORM_EPS)


POOL_HALO = 16


def _shift_rows_down(s, shift):
    if shift == F32_SUBLANES:
        return jnp.concatenate([s[:shift], s[:-shift]], axis=0)
    return pltpu.roll(s, shift, axis=0)


def _pool_group(prev, u, window, is_seq_start):
    s = jnp.concatenate([prev, u], axis=0)
    shift = 1
    while shift < window:
        s = s + _shift_rows_down(s, shift)
        shift *= 2
    s = s[POOL_HALO:]
    pooled = s * (1.0 / window) - u
    cnt = (lax.broadcasted_iota(jnp.int32, (POOL_HALO, u.shape[1]), 0) + 1).astype(jnp.float32)
    cnt = jnp.where(is_seq_start, jnp.minimum(cnt, float(window)), float(window))
    top = s[:POOL_HALO] / cnt - u[:POOL_HALO]
    return jnp.concatenate([top, pooled[POOL_HALO:]], axis=0)


def _norm_pool_kernel(x_ref, g_ref, w_ref, wp_ref, ps_ref, h_ref, o_ref,
                      wbf_ref, wpb_ref, halo_ref, *, tiles_per_seq):
    i = pl.program_id(0)

    @pl.when(i == 0)
    def _():
        wbf_ref[...] = w_ref[...].astype(jnp.bfloat16)
        wpb_ref[...] = wp_ref[...].astype(jnp.bfloat16)
        halo_ref[...] = jnp.zeros_like(halo_ref)

    parts = []
    for rows in _row_subtiles(x_ref.shape[0]):
        h = (_rms_scale(x_ref[rows, :]) * g_ref[...]).astype(jnp.bfloat16)
        h_ref[rows, :] = h
        parts.append(jnp.dot(h, wbf_ref[...], preferred_element_type=jnp.float32))
    u = jnp.concatenate(parts, axis=0)

    is_seq_start = (i % tiles_per_seq) == 0
    prev = jnp.where(is_seq_start, 0.0, halo_ref[...])
    halo_ref[...] = u[-POOL_HALO:]
    cg = wp_ref.shape[1]
    for gi, window in enumerate(POOL_WINDOWS):
        cols = slice(gi * cg, (gi + 1) * cg)
        pooled = _pool_group(prev[:, cols], u[:, cols], window, is_seq_start)
        mapped = jnp.dot(pooled.astype(jnp.bfloat16), wpb_ref[gi],
                         preferred_element_type=jnp.float32)
        o_ref[:, cols] = (mapped * ps_ref[:, cols]).astype(o_ref.dtype)


def _norm_pool(x2d, g, w, w_pool, pool_scale, seq, tm=512):
    m, d = x2d.shape
    ng, cg, _ = w_pool.shape
    pw = ng * cg
    assert seq % tm == 0 and max(POOL_WINDOWS) <= POOL_HALO
    return pl.pallas_call(
        functools.partial(_norm_pool_kernel, tiles_per_seq=seq // tm),
        grid=(m // tm,),
        in_specs=[pl.BlockSpec((tm, d), lambda i: (i, 0)),
                  pl.BlockSpec((1, d), lambda i: (0, 0)),
                  pl.BlockSpec((d, pw), lambda i: (0, 0), pipeline_mode=pl.Buffered(1)),
                  pl.BlockSpec((ng, cg, cg), lambda i: (0, 0, 0), pipeline_mode=pl.Buffered(1)),
                  pl.BlockSpec((1, pw), lambda i: (0, 0))],
        out_specs=[pl.BlockSpec((tm, d), lambda i: (i, 0)),
                   pl.BlockSpec((tm, pw), lambda i: (i, 0))],
        out_shape=[jax.ShapeDtypeStruct((m, d), jnp.bfloat16),
                   jax.ShapeDtypeStruct((m, pw), jnp.bfloat16)],
        scratch_shapes=[pltpu.VMEM((d, pw), jnp.bfloat16),
                        pltpu.VMEM((ng, cg, cg), jnp.bfloat16),
                        pltpu.VMEM((POOL_HALO, pw), jnp.float32)],
        compiler_params=_params(("arbitrary",)),
        name="norm_pool",
    )(x2d, g.reshape(1, d), w, w_pool, pool_scale.reshape(1, pw))


def _rope(t, cos_t, sin_t):
    width = t.shape[-1]
    half = HEAD_DIM // 2
    lane = lax.broadcasted_iota(jnp.int32, t.shape, 1)
    first_half = (lane % HEAD_DIM) < half
    partner = jnp.where(first_half,
                        pltpu.roll(t, width - half, axis=1),
                        pltpu.roll(t, half, axis=1))
    return t * cos_t + partner * sin_t


def _proj_attn_kernel(h_ref, wq_ref, wk_ref, wv_ref, cos_ref, sin_ref, lq1_ref, lk1_ref,
                      lq2_ref, lk2_ref, gain_ref, wo_src_ref, wd_src_ref,
                      o_ref, wo_dst_ref, wd_dst_ref,
                      wbf_ref, q_ref, k_ref, vt_ref, *st_refs, tq, rb):
    seq = h_ref.shape[1]
    hw = HEAD_WIDTH
    gw = q_ref.shape[1]
    heads = gw // hw
    n_tiles = seq // tq
    kc = tq
    n_row_blocks = seq // rb

    wo_dst_ref[...] = wo_src_ref[...].astype(jnp.bfloat16)
    wd_dst_ref[...] = wd_src_ref[...].astype(jnp.bfloat16)

    wbf_ref[:, :gw] = wq_ref[...].astype(jnp.bfloat16)
    wbf_ref[:, gw:2 * gw] = wk_ref[...].astype(jnp.bfloat16)
    wbf_ref[:, 2 * gw:] = wv_ref[...].astype(jnp.bfloat16)

    for hd in range(heads):
        vt_ref[hd, hw:, :] = jnp.ones((vt_ref.shape[1] - hw, seq), jnp.bfloat16)

    lam = (jnp.exp(jnp.sum(lq1_ref[...] * lk1_ref[...], axis=-1, keepdims=True))
           - jnp.exp(jnp.sum(lq2_ref[...] * lk2_ref[...], axis=-1, keepdims=True))
           + LAMBDA_INIT)

    lane = lax.broadcasted_iota(jnp.int32, (tq, hw), 1)
    key = lax.broadcasted_iota(jnp.int32, (kc, 2 * tq), 0)
    qry = lax.broadcasted_iota(jnp.int32, (kc, 2 * tq), 1) % tq

    def project(r):
        rows = slice(r * rb, (r + 1) * rb)
        y = jnp.dot(h_ref[0, rows, :], wbf_ref[...], preferred_element_type=jnp.float32)
        cos_t, sin_t = cos_ref[rows, :], sin_ref[rows, :]
        q_scale = LOG2_E / math.sqrt(HEAD_DIM)
        cos_q, sin_q = cos_t * q_scale, sin_t * q_scale
        for hd in range(heads):
            cols = slice(hd * hw, (hd + 1) * hw)
            q_ref[rows, cols] = _rope(y[:, cols], cos_q, sin_q).astype(jnp.bfloat16)
            k_ref[rows, cols] = _rope(y[:, gw + hd * hw:gw + (hd + 1) * hw],
                                      cos_t, sin_t).astype(jnp.bfloat16)
            vt_ref[hd, :hw, rows] = y[:, 2 * gw + hd * hw:2 * gw + (hd + 1) * hw
                                      ].T.astype(jnp.bfloat16)

    projected = []

    def ensure_projected(r):
        while len(projected) <= min(r, n_row_blocks - 1):
            project(len(projected))
            projected.append(True)

    def masked_queries(hd, i):
        q = q_ref[i * tq:(i + 1) * tq, hd * hw:(hd + 1) * hw]
        return jnp.concatenate([jnp.where(lane < HEAD_DIM, q, 0),
                                jnp.where(lane >= HEAD_DIM, q, 0)], axis=0)

    def scores(hd, i, c, q2, slot):
        s = lax.dot_general(k_ref[c * kc:(c + 1) * kc, hd * hw:(hd + 1) * hw], q2,
                            (((1,), (1,)), ((), ())),
                            preferred_element_type=jnp.float32)
        if c == i:
            s = jnp.where((qry // CHUNK) >= (key // CHUNK), s, -jnp.inf)
        st_refs[slot][...] = s
        return jnp.max(s, axis=0, keepdims=True)

    def values(hd, c, m, slot):
        p = jnp.exp2(st_refs[slot][...] - m).astype(jnp.bfloat16)
        return jnp.dot(vt_ref[hd, :, c * kc:(c + 1) * kc], p,
                       preferred_element_type=jnp.float32)

    items = [(hd, i, c) for i in range(n_tiles) for c in range(i + 1) for hd in range(heads)]
    lookahead = len(st_refs) - 1
    q2s, refs, accs = {}, {}, {}
    m_run = {}

    def block_of(i):
        return ((i + 1) * tq - 1) // rb

    def emit_scores(n):
        hd, i, c = items[n]
        ensure_projected(block_of(i))
        if (hd, i) not in q2s:
            q2s[hd, i] = masked_queries(hd, i)
        m_c = scores(hd, i, c, q2s[hd, i], n % len(st_refs))
        if c == 0:
            refs[n] = (m_c, None)
        else:
            m_new = jnp.maximum(m_run[hd], m_c)
            refs[n] = (m_new, jnp.exp2(m_run[hd] - m_new))
        m_run[hd] = refs[n][0]

    ensure_projected(1)
    for n in range(min(lookahead, len(items))):
        emit_scores(n)
    for n, (hd, i, c) in enumerate(items):
        ensure_projected(block_of(i) + 1)
        if n + lookahead < len(items):
            emit_scores(n + lookahead)
        m_new, alpha = refs.pop(n)
        part = values(hd, c, m_new, n % len(st_refs))
        accs[hd] = part if c == 0 else accs[hd] * alpha + part
        if c < i:
            continue
        acc = accs.pop(hd)
        ot = acc[:hw] * (1.0 / acc[hw:hw + 1])
        o = (ot[:, :tq] - lam * ot[:, tq:]).T
        o = _rms_scale(o) * gain_ref[...] * (1.0 - LAMBDA_INIT)
        o_ref[0, i * tq:(i + 1) * tq, hd * hw:(hd + 1) * hw] = o.astype(o_ref.dtype)


def _proj_attention(h3, w_in, cos_t, sin_t, lq1, lk1, lq2, lk2, gain, w_out, w_down, *,
                    q_col0, n_heads, heads_per_step=2, tq=256, rb=512, n_slots=4):
    b, s, d = h3.shape
    hw = HEAD_WIDTH
    gw = heads_per_step * hw
    width = n_heads * hw
    nj = n_heads // heads_per_step
    assert q_col0 % gw == 0 and width % gw == 0 and s % rb == 0 and rb % tq == 0
    qb, kb, vb = q_col0 // gw, (q_col0 + width) // gw, (q_col0 + 2 * width) // gw
    steps = b * nj
    wo_slab, wo_rem = divmod(w_out.shape[0], steps)
    wd_slab, wd_rem = divmod(w_down.shape[0], steps)
    assert wo_rem == 0 and wd_rem == 0
    assert wo_slab % BF16_SUBLANES == 0 and wd_slab % BF16_SUBLANES == 0
    vec = lambda a: a.reshape(1, -1)
    small = lambda n: pl.BlockSpec((1, n), lambda bi, j: (0, 0))
    table = pl.BlockSpec((s, hw), lambda bi, j: (0, 0), pipeline_mode=pl.Buffered(1))
    slab = lambda rows, cols: pl.BlockSpec((rows, cols), lambda bi, j: (bi * nj + j, 0))
    return pl.pallas_call(
        functools.partial(_proj_attn_kernel, tq=tq, rb=rb),
        grid=(b, nj),
        in_specs=[pl.BlockSpec((1, s, d), lambda bi, j: (bi, 0, 0)),
                  pl.BlockSpec((d, gw), lambda bi, j: (0, qb + j)),
                  pl.BlockSpec((d, gw), lambda bi, j: (0, kb + j)),
                  pl.BlockSpec((d, gw), lambda bi, j: (0, vb + j)),
                  table, table,
                  small(HEAD_DIM), small(HEAD_DIM), small(HEAD_DIM), small(HEAD_DIM),
                  small(hw),
                  slab(wo_slab, w_out.shape[1]), slab(wd_slab, w_down.shape[1])],
        out_specs=[pl.BlockSpec((1, s, gw), lambda bi, j: (bi, 0, j)),
                   slab(wo_slab, w_out.shape[1]), slab(wd_slab, w_down.shape[1])],
        out_shape=[jax.ShapeDtypeStruct((b, s, width), jnp.bfloat16),
                   jax.ShapeDtypeStruct(w_out.shape, jnp.bfloat16),
                   jax.ShapeDtypeStruct(w_down.shape, jnp.bfloat16)],
        scratch_shapes=[pltpu.VMEM((d, 3 * gw), jnp.bfloat16),
                        pltpu.VMEM((s, gw), jnp.bfloat16),
                        pltpu.VMEM((s, gw), jnp.bfloat16),
                        pltpu.VMEM((heads_per_step, hw + BF16_SUBLANES, s), jnp.bfloat16)]
        + [pltpu.VMEM((tq, 2 * tq), jnp.float32)] * n_slots,
        compiler_params=_params(("arbitrary", "arbitrary")),
        name="proj_attn",
    )(h3, w_in, w_in, w_in, cos_t, sin_t, vec(lq1), vec(lk1), vec(lq2), vec(lk2), vec(gain),
      w_out, w_down)


RING_DEPTH = 3


def _out_proj_kernel(p_ref, a_ref, x_hbm_ref, w_ref, g_ref, x1_ref, h_ref, xbuf_ref, sem_ref):
    i = pl.program_id(0)
    n_steps = pl.num_programs(0)
    tm = xbuf_ref.shape[1]
    kp = p_ref.shape[1]

    def x_copy(tile, slot):
        return pltpu.make_async_copy(x_hbm_ref.at[pl.ds(tile * tm, tm), :],
                                     xbuf_ref.at[slot], sem_ref.at[slot])

    @pl.when(i == 0)
    def _():
        for t in range(RING_DEPTH - 1):
            @pl.when(t < n_steps)
            def _(t=t):
                x_copy(t, t).start()

    ahead = i + (RING_DEPTH - 1)

    @pl.when(ahead < n_steps)
    def _():
        x_copy(ahead, ahead % RING_DEPTH).start()

    slot = i % RING_DEPTH
    x_copy(i, slot).wait()
    for rows in _row_subtiles(tm):
        y = jnp.dot(p_ref[rows, :], w_ref[:kp, :], preferred_element_type=jnp.float32)
        y += jnp.dot(a_ref[rows, :], w_ref[kp:, :], preferred_element_type=jnp.float32)
        x1_ref[rows, :] = xbuf_ref[slot, rows, :] + y
        h_ref[rows, :] = (_rms_scale(x1_ref[rows, :]) * g_ref[...]).astype(h_ref.dtype)


def _out_proj(pool_out, attn_out, x2d, w_out_bf16, g, tm=512):
    m, d = x2d.shape
    kp, ka = pool_out.shape[1], attn_out.shape[1]
    return pl.pallas_call(
        _out_proj_kernel,
        grid=(m // tm,),
        in_specs=[pl.BlockSpec((tm, kp), lambda i: (i, 0)),
                  pl.BlockSpec((tm, ka), lambda i: (i, 0)),
                  pl.BlockSpec(memory_space=pl.ANY),
                  pl.BlockSpec((kp + ka, d), lambda i: (0, 0),
                               pipeline_mode=pl.Buffered(1)),
                  pl.BlockSpec((1, d), lambda i: (0, 0))],
        out_specs=[pl.BlockSpec((tm, d), lambda i: (i, 0)),
                   pl.BlockSpec((tm, d), lambda i: (i, 0))],
        out_shape=[jax.ShapeDtypeStruct((m, d), jnp.float32),
                   jax.ShapeDtypeStruct((m, d), jnp.bfloat16)],
        scratch_shapes=[pltpu.VMEM((RING_DEPTH, tm, d), jnp.float32),
                        pltpu.SemaphoreType.DMA((RING_DEPTH,))],
        compiler_params=_params(("arbitrary",)),
        name="out_proj",
    )(pool_out, attn_out, x2d, w_out_bf16, g.reshape(1, d))


def _gate_up_kernel(a_ref, wg_ref, wu_ref, o_ref, wgb_ref, wub_ref):
    @pl.when(pl.program_id(1) == 0)
    def _():
        wgb_ref[...] = wg_ref[...].astype(jnp.bfloat16)
        wub_ref[...] = wu_ref[...].astype(jnp.bfloat16)

    for rows in _row_subtiles(a_ref.shape[0]):
        a = a_ref[rows, :]
        gate = jnp.dot(a, wgb_ref[...], preferred_element_type=jnp.float32)
        up = jnp.dot(a, wub_ref[...], preferred_element_type=jnp.float32)
        o_ref[rows, :] = (gate * jax.nn.sigmoid(gate) * up).astype(o_ref.dtype)


def _gate_up(a, w_gate_up, hidden, tm=2048, th=512):
    m, k = a.shape
    nb = hidden // th
    return pl.pallas_call(
        _gate_up_kernel,
        grid=(nb, m // tm),
        in_specs=[pl.BlockSpec((tm, k), lambda j, i: (i, 0)),
                  pl.BlockSpec((k, th), lambda j, i: (0, j)),
                  pl.BlockSpec((k, th), lambda j, i: (0, j + nb))],
        out_specs=pl.BlockSpec((tm, th), lambda j, i: (i, j)),
        out_shape=jax.ShapeDtypeStruct((m, hidden), jnp.bfloat16),
        scratch_shapes=[pltpu.VMEM((k, th), jnp.bfloat16),
                        pltpu.VMEM((k, th), jnp.bfloat16)],
        compiler_params=_params(("arbitrary", "arbitrary")),
        name="gate_up",
    )(a, w_gate_up, w_gate_up)


def _down_kernel(a_ref, w_ref, x1_ref, g_ref, o_ref):
    for rows in _row_subtiles(o_ref.shape[0]):
        o_ref[rows, :] = x1_ref[rows, :] + jnp.dot(a_ref[rows, :], w_ref[...],
                                                   preferred_element_type=jnp.float32)
        o_ref[rows, :] = _rms_scale(o_ref[rows, :]) * g_ref[...]


def _down(a, w_down_bf16, x1, g, tm=512):
    m, hidden = a.shape
    d = w_down_bf16.shape[1]
    return pl.pallas_call(
        _down_kernel,
        grid=(m // tm,),
        in_specs=[pl.BlockSpec((tm, hidden), lambda i: (i, 0)),
                  pl.BlockSpec((hidden, d), lambda i: (0, 0), pipeline_mode=pl.Buffered(1)),
                  pl.BlockSpec((tm, d), lambda i: (i, 0)),
                  pl.BlockSpec((1, d), lambda i: (0, 0))],
        out_specs=pl.BlockSpec((tm, d), lambda i: (i, 0)),
        out_shape=jax.ShapeDtypeStruct((m, d), jnp.float32),
        compiler_params=_params(("arbitrary",)),
        name="down",
    )(a, w_down_bf16, x1, g.reshape(1, d))


def _rotary_tables(seq):
    inv_freq = ROPE_THETA ** (-jnp.arange(0, HEAD_DIM, 2, dtype=jnp.float32) / HEAD_DIM)
    ang = jnp.arange(seq, dtype=jnp.float32)[:, None] * inv_freq[None, :]
    cos, sin = jnp.cos(ang), jnp.sin(ang)
    reps = HEAD_WIDTH // HEAD_DIM
    cos_t = jnp.tile(jnp.concatenate([cos, cos], axis=-1), (1, reps))
    sin_t = jnp.tile(jnp.concatenate([-sin, sin], axis=-1), (1, reps))
    return cos_t, sin_t


def kernel(x, norm_mix, w_in, w_pool, pool_scale, lambda_q1, lambda_k1, lambda_q2,
           lambda_k2, subln_gain, w_out, norm_ffn, w_gate_up, w_down, norm_final):
    b, s, d = x.shape
    depth = w_in.shape[0]
    assert depth == 1, "single-layer block"
    pool_width = w_pool.shape[1] * w_pool.shape[2]
    attn_width = w_out.shape[1] - pool_width
    n_heads = attn_width // HEAD_WIDTH

    cos_t, sin_t = _rotary_tables(s)
    x2d = x.reshape(b * s, d)
    l = 0
    h1, pool_out = _norm_pool(x2d, norm_mix[l], w_in[l], w_pool[l], pool_scale[l], s)
    attn_out, w_out_bf16, w_down_bf16 = _proj_attention(
        h1.reshape(b, s, d), w_in[l], cos_t, sin_t, lambda_q1[l], lambda_k1[l], lambda_q2[l],
        lambda_k2[l], subln_gain[l], w_out[l], w_down[l], q_col0=pool_width, n_heads=n_heads)
    x1, h2 = _out_proj(pool_out, attn_out.reshape(b * s, -1), x2d,
                       w_out_bf16, norm_ffn[l])
    a = _gate_up(h2, w_gate_up[l], w_down.shape[1])
    out = _down(a, w_down_bf16, x1, norm_final)
    return out.reshape(b, s, d)
```

```python
import functools
import math

import jax
import jax.numpy as jnp
from jax import lax
from jax.experimental import pallas as pl
from jax.experimental.pallas import tpu as pltpu

CHUNK = 64
POOL_WINDOWS = (2, 4, 8, 16)
HEAD_DIM = 64
HEAD_WIDTH = 2 * HEAD_DIM
ROPE_THETA = 10000.0
NORM_EPS = 1e-6
LAMBDA_INIT = 0.8 - 0.6 * math.exp(-0.3 * 0)

V7X_VMEM_LIMIT_BYTES = 56 * 1024 * 1024

F32_SUBLANES = 8
BF16_SUBLANES = 16

LOG2_E = math.log2(math.e)
PROJ_AHEAD = 2


def _params(semantics):
    return pltpu.CompilerParams(dimension_semantics=semantics,
                                vmem_limit_bytes=V7X_VMEM_LIMIT_BYTES)


def _row_subtiles(rows, n=2):
    step, rem = divmod(rows, n)
    assert rem == 0
    return [slice(r * step, (r + 1) * step) for r in range(n)]


def _rms_scale(x):
    return x * lax.rsqrt(jnp.mean(x * x, axis=-1, keepdims=True) + NORM_EPS)


POOL_HALO = 16


def _shift_rows_down(s, shift):
    if shift == F32_SUBLANES:
        return jnp.concatenate([s[:shift], s[:-shift]], axis=0)
    return pltpu.roll(s, shift, axis=0)


def _pool_group(prev, u, window, is_seq_start):
    s = jnp.concatenate([prev, u], axis=0)
    shift = 1
    while shift < window:
        s = s + _shift_rows_down(s, shift)
        shift *= 2
    s = s[POOL_HALO:]
    pooled = s * (1.0 / window) - u
    cnt = (lax.broadcasted_iota(jnp.int32, (POOL_HALO, u.shape[1]), 0) + 1).astype(jnp.float32)
    cnt = jnp.where(is_seq_start, jnp.minimum(cnt, float(window)), float(window))
    top = s[:POOL_HALO] / cnt - u[:POOL_HALO]
    return jnp.concatenate([top, pooled[POOL_HALO:]], axis=0)


def _norm_pool_kernel(x_ref, g_ref, w_ref, wp_ref, ps_ref, h_ref, o_ref,
                      wbf_ref, wpb_ref, halo_ref, *, tiles_per_seq):
    i = pl.program_id(0)

    @pl.when(i == 0)
    def _():
        wbf_ref[...] = w_ref[...].astype(jnp.bfloat16)
        wpb_ref[...] = wp_ref[...].astype(jnp.bfloat16)
        halo_ref[...] = jnp.zeros_like(halo_ref)

    parts = []
    for rows in _row_subtiles(x_ref.shape[0]):
        h = (_rms_scale(x_ref[rows, :]) * g_ref[...]).astype(jnp.bfloat16)
        h_ref[rows, :] = h
        parts.append(jnp.dot(h, wbf_ref[...], preferred_element_type=jnp.float32))
    u = jnp.concatenate(parts, axis=0)

    is_seq_start = (i % tiles_per_seq) == 0
    prev = jnp.where(is_seq_start, 0.0, halo_ref[...])
    halo_ref[...] = u[-POOL_HALO:]
    cg = wp_ref.shape[1]
    for gi, window in enumerate(POOL_WINDOWS):
        cols = slice(gi * cg, (gi + 1) * cg)
        pooled = _pool_group(prev[:, cols], u[:, cols], window, is_seq_start)
        mapped = jnp.dot(pooled.astype(jnp.bfloat16), wpb_ref[gi],
                         preferred_element_type=jnp.float32)
        o_ref[:, cols] = (mapped * ps_ref[:, cols]).astype(o_ref.dtype)


def _norm_pool(x2d, g, w, w_pool, pool_scale, seq, tm=1024):
    m, d = x2d.shape
    ng, cg, _ = w_pool.shape
    pw = ng * cg
    assert seq % tm == 0 and max(POOL_WINDOWS) <= POOL_HALO
    return pl.pallas_call(
        functools.partial(_norm_pool_kernel, tiles_per_seq=seq // tm),
        grid=(m // tm,),
        in_specs=[pl.BlockSpec((tm, d), lambda i: (i, 0)),
                  pl.BlockSpec((1, d), lambda i: (0, 0)),
                  pl.BlockSpec((d, pw), lambda i: (0, 0), pipeline_mode=pl.Buffered(1)),
                  pl.BlockSpec((ng, cg, cg), lambda i: (0, 0, 0), pipeline_mode=pl.Buffered(1)),
                  pl.BlockSpec((1, pw), lambda i: (0, 0))],
        out_specs=[pl.BlockSpec((tm, d), lambda i: (i, 0)),
                   pl.BlockSpec((tm, pw), lambda i: (i, 0))],
        out_shape=[jax.ShapeDtypeStruct((m, d), jnp.bfloat16),
                   jax.ShapeDtypeStruct((m, pw), jnp.bfloat16)],
        scratch_shapes=[pltpu.VMEM((d, pw), jnp.bfloat16),
                        pltpu.VMEM((ng, cg, cg), jnp.bfloat16),
                        pltpu.VMEM((POOL_HALO, pw), jnp.float32)],
        compiler_params=_params(("arbitrary",)),
        name="norm_pool",
    )(x2d, g.reshape(1, d), w, w_pool, pool_scale.reshape(1, pw))


def _rope(t, cos_t, sin_t):
    width = t.shape[-1]
    half = HEAD_DIM // 2
    lane = lax.broadcasted_iota(jnp.int32, t.shape, 1)
    first_half = (lane % HEAD_DIM) < half
    partner = jnp.where(first_half,
                        pltpu.roll(t, width - half, axis=1),
                        pltpu.roll(t, half, axis=1))
    return t * cos_t + partner * sin_t


def _proj_attn_kernel(h_ref, wq_ref, wk_ref, wv_ref, cos_ref, sin_ref, lq1_ref, lk1_ref,
                      lq2_ref, lk2_ref, gain_ref, wo_src_ref, wd_src_ref,
                      o_ref, wo_dst_ref, wd_dst_ref,
                      wbf_ref, q_ref, k_ref, vt_ref, *st_refs, tq, rb):
    seq = h_ref.shape[1]
    hw = HEAD_WIDTH
    gw = q_ref.shape[1]
    heads = gw // hw
    n_tiles = seq // tq
    kc = tq
    n_row_blocks = seq // rb

    wo_dst_ref[...] = wo_src_ref[...].astype(jnp.bfloat16)
    wd_dst_ref[...] = wd_src_ref[...].astype(jnp.bfloat16)

    wbf_ref[:, :gw] = wq_ref[...].astype(jnp.bfloat16)
    wbf_ref[:, gw:2 * gw] = wk_ref[...].astype(jnp.bfloat16)
    wbf_ref[:, 2 * gw:] = wv_ref[...].astype(jnp.bfloat16)

    for hd in range(heads):
        vt_ref[hd, hw:, :] = jnp.ones((vt_ref.shape[1] - hw, seq), jnp.bfloat16)

    lam = (jnp.exp(jnp.sum(lq1_ref[...] * lk1_ref[...], axis=-1, keepdims=True))
           - jnp.exp(jnp.sum(lq2_ref[...] * lk2_ref[...], axis=-1, keepdims=True))
           + LAMBDA_INIT)

    lane = lax.broadcasted_iota(jnp.int32, (tq, hw), 1)
    key = lax.broadcasted_iota(jnp.int32, (kc, 2 * tq), 0)
    qry = lax.broadcasted_iota(jnp.int32, (kc, 2 * tq), 1) % tq

    def project(r):
        rows = slice(r * rb, (r + 1) * rb)
        y = jnp.dot(h_ref[0, rows, :], wbf_ref[...], preferred_element_type=jnp.float32)
        cos_t, sin_t = cos_ref[rows, :], sin_ref[rows, :]
        q_scale = LOG2_E / math.sqrt(HEAD_DIM)
        cos_q, sin_q = cos_t * q_scale, sin_t * q_scale
        for hd in range(heads):
            cols = slice(hd * hw, (hd + 1) * hw)
            q_ref[rows, cols] = _rope(y[:, cols], cos_q, sin_q).astype(jnp.bfloat16)
            k_ref[rows, cols] = _rope(y[:, gw + hd * hw:gw + (hd + 1) * hw],
                                      cos_t, sin_t).astype(jnp.bfloat16)
            vt_ref[hd, :hw, rows] = y[:, 2 * gw + hd * hw:2 * gw + (hd + 1) * hw
                                      ].T.astype(jnp.bfloat16)

    projected = []

    def ensure_projected(r):
        while len(projected) <= min(r, n_row_blocks - 1):
            project(len(projected))
            projected.append(True)

    def masked_queries(hd, i):
        q = q_ref[i * tq:(i + 1) * tq, hd * hw:(hd + 1) * hw]
        return jnp.concatenate([jnp.where(lane < HEAD_DIM, q, 0),
                                jnp.where(lane >= HEAD_DIM, q, 0)], axis=0)

    def scores(hd, i, c, q2, slot):
        s = lax.dot_general(k_ref[c * kc:(c + 1) * kc, hd * hw:(hd + 1) * hw], q2,
                            (((1,), (1,)), ((), ())),
                            preferred_element_type=jnp.float32)
        if c == i:
            s = jnp.where((qry // CHUNK) >= (key // CHUNK), s, -jnp.inf)
        st_refs[slot][...] = s
        return jnp.max(s, axis=0, keepdims=True)

    def values(hd, c, m, slot):
        p = jnp.exp2(st_refs[slot][...] - m).astype(jnp.bfloat16)
        return jnp.dot(vt_ref[hd, :, c * kc:(c + 1) * kc], p,
                       preferred_element_type=jnp.float32)

    items = [(hd, i, c) for i in range(n_tiles) for c in range(i + 1) for hd in range(heads)]
    lookahead = len(st_refs) - 1
    q2s, refs, accs = {}, {}, {}
    m_run = {}

    def block_of(i):
        return ((i + 1) * tq - 1) // rb

    def emit_scores(n):
        hd, i, c = items[n]
        ensure_projected(block_of(i))
        if (hd, i) not in q2s:
            q2s[hd, i] = masked_queries(hd, i)
        m_c = scores(hd, i, c, q2s[hd, i], n % len(st_refs))
        if c == 0:
            refs[n] = (m_c, None)
        else:
            m_new = jnp.maximum(m_run[hd], m_c)
            refs[n] = (m_new, jnp.exp2(m_run[hd] - m_new))
        m_run[hd] = refs[n][0]

    ensure_projected(PROJ_AHEAD)
    for n in range(min(lookahead, len(items))):
        emit_scores(n)
    for n, (hd, i, c) in enumerate(items):
        ensure_projected(block_of(i) + PROJ_AHEAD)
        if n + lookahead < len(items):
            emit_scores(n + lookahead)
        m_new, alpha = refs.pop(n)
        part = values(hd, c, m_new, n % len(st_refs))
        accs[hd] = part if c == 0 else accs[hd] * alpha + part
        if c < i:
            continue
        acc = accs.pop(hd)
        ot = acc[:hw] * (1.0 / acc[hw:hw + 1])
        o = (ot[:, :tq] - lam * ot[:, tq:]).T
        o = _rms_scale(o) * gain_ref[...] * (1.0 - LAMBDA_INIT)
        o_ref[0, i * tq:(i + 1) * tq, hd * hw:(hd + 1) * hw] = o.astype(o_ref.dtype)


def _proj_attention(h3, w_in, cos_t, sin_t, lq1, lk1, lq2, lk2, gain, w_out, w_down, *,
                    q_col0, n_heads, heads_per_step=2, tq=256, rb=256, n_slots=4):
    b, s, d = h3.shape
    hw = HEAD_WIDTH
    gw = heads_per_step * hw
    width = n_heads * hw
    nj = n_heads // heads_per_step
    assert q_col0 % gw == 0 and width % gw == 0 and s % rb == 0 and rb % tq == 0
    qb, kb, vb = q_col0 // gw, (q_col0 + width) // gw, (q_col0 + 2 * width) // gw
    steps = b * nj
    wo_slab, wo_rem = divmod(w_out.shape[0], steps)
    wd_slab, wd_rem = divmod(w_down.shape[0], steps)
    assert wo_rem == 0 and wd_rem == 0
    assert wo_slab % BF16_SUBLANES == 0 and wd_slab % BF16_SUBLANES == 0
    vec = lambda a: a.reshape(1, -1)
    small = lambda n: pl.BlockSpec((1, n), lambda bi, j: (0, 0))
    table = pl.BlockSpec((s, hw), lambda bi, j: (0, 0), pipeline_mode=pl.Buffered(1))
    slab = lambda rows, cols: pl.BlockSpec((rows, cols), lambda bi, j: (bi * nj + j, 0))
    return pl.pallas_call(
        functools.partial(_proj_attn_kernel, tq=tq, rb=rb),
        grid=(b, nj),
        in_specs=[pl.BlockSpec((1, s, d), lambda bi, j: (bi, 0, 0)),
                  pl.BlockSpec((d, gw), lambda bi, j: (0, qb + j)),
                  pl.BlockSpec((d, gw), lambda bi, j: (0, kb + j)),
                  pl.BlockSpec((d, gw), lambda bi, j: (0, vb + j)),
                  table, table,
                  small(HEAD_DIM), small(HEAD_DIM), small(HEAD_DIM), small(HEAD_DIM),
                  small(hw),
                  slab(wo_slab, w_out.shape[1]), slab(wd_slab, w_down.shape[1])],
        out_specs=[pl.BlockSpec((1, s, gw), lambda bi, j: (bi, 0, j)),
                   slab(wo_slab, w_out.shape[1]), slab(wd_slab, w_down.shape[1])],
        out_shape=[jax.ShapeDtypeStruct((b, s, width), jnp.bfloat16),
                   jax.ShapeDtypeStruct(w_out.shape, jnp.bfloat16),
                   jax.ShapeDtypeStruct(w_down.shape, jnp.bfloat16)],
        scratch_shapes=[pltpu.VMEM((d, 3 * gw), jnp.bfloat16),
                        pltpu.VMEM((s, gw), jnp.bfloat16),
                        pltpu.VMEM((s, gw), jnp.bfloat16),
                        pltpu.VMEM((heads_per_step, hw + BF16_SUBLANES, s), jnp.bfloat16)]
        + [pltpu.VMEM((tq, 2 * tq), jnp.float32)] * n_slots,
        compiler_params=_params(("arbitrary", "arbitrary")),
        name="proj_attn",
    )(h3, w_in, w_in, w_in, cos_t, sin_t, vec(lq1), vec(lk1), vec(lq2), vec(lk2), vec(gain),
      w_out, w_down)


def _out_proj_kernel(p_ref, a_ref, x_ref, w_ref, g_ref, x1_ref, h_ref):
    kp = p_ref.shape[1]
    for rows in _row_subtiles(x_ref.shape[0]):
        y = jnp.dot(p_ref[rows, :], w_ref[:kp, :], preferred_element_type=jnp.float32)
        y += jnp.dot(a_ref[rows, :], w_ref[kp:, :], preferred_element_type=jnp.float32)
        x1_ref[rows, :] = x_ref[rows, :] + y
        h_ref[rows, :] = (_rms_scale(x1_ref[rows, :]) * g_ref[...]).astype(h_ref.dtype)


def _out_proj(pool_out, attn_out, x2d, w_out_bf16, g, tm=512):
    m, d = x2d.shape
    kp, ka = pool_out.shape[1], attn_out.shape[1]
    return pl.pallas_call(
        _out_proj_kernel,
        grid=(m // tm,),
        in_specs=[pl.BlockSpec((tm, kp), lambda i: (i, 0)),
                  pl.BlockSpec((tm, ka), lambda i: (i, 0)),
                  pl.BlockSpec((tm, d), lambda i: (i, 0)),
                  pl.BlockSpec((kp + ka, d), lambda i: (0, 0),
                               pipeline_mode=pl.Buffered(1)),
                  pl.BlockSpec((1, d), lambda i: (0, 0))],
        out_specs=[pl.BlockSpec((tm, d), lambda i: (i, 0)),
                   pl.BlockSpec((tm, d), lambda i: (i, 0))],
        out_shape=[jax.ShapeDtypeStruct((m, d), jnp.float32),
                   jax.ShapeDtypeStruct((m, d), jnp.bfloat16)],
        compiler_params=_params(("arbitrary",)),
        name="out_proj",
    )(pool_out, attn_out, x2d, w_out_bf16, g.reshape(1, d))


def _gate_up_kernel(a_ref, wg_ref, wu_ref, o_ref, wgb_ref, wub_ref):
    @pl.when(pl.program_id(1) == 0)
    def _():
        wgb_ref[...] = wg_ref[...].astype(jnp.bfloat16)
        wub_ref[...] = wu_ref[...].astype(jnp.bfloat16)

    for rows in _row_subtiles(a_ref.shape[0]):
        a = a_ref[rows, :]
        gate = jnp.dot(a, wgb_ref[...], preferred_element_type=jnp.float32)
        up = jnp.dot(a, wub_ref[...], preferred_element_type=jnp.float32)
        o_ref[rows, :] = (gate * jax.nn.sigmoid(gate) * up).astype(o_ref.dtype)


def _gate_up(a, w_gate_up, hidden, tm=2048, th=512):
    m, k = a.shape
    nb = hidden // th
    return pl.pallas_call(
        _gate_up_kernel,
        grid=(nb, m // tm),
        in_specs=[pl.BlockSpec((tm, k), lambda j, i: (i, 0)),
                  pl.BlockSpec((k, th), lambda j, i: (0, j)),
                  pl.BlockSpec((k, th), lambda j, i: (0, j + nb))],
        out_specs=pl.BlockSpec((tm, th), lambda j, i: (i, j)),
        out_shape=jax.ShapeDtypeStruct((m, hidden), jnp.bfloat16),
        scratch_shapes=[pltpu.VMEM((k, th), jnp.bfloat16),
                        pltpu.VMEM((k, th), jnp.bfloat16)],
        compiler_params=_params(("arbitrary", "arbitrary")),
        name="gate_up",
    )(a, w_gate_up, w_gate_up)


def _down_kernel(a_ref, w_ref, x1_ref, g_ref, o_ref):
    for rows in _row_subtiles(o_ref.shape[0]):
        o_ref[rows, :] = x1_ref[rows, :] + jnp.dot(a_ref[rows, :], w_ref[...],
                                                   preferred_element_type=jnp.float32)
        o_ref[rows, :] = _rms_scale(o_ref[rows, :]) * g_ref[...]


def _down(a, w_down_bf16, x1, g, tm=512):
    m, hidden = a.shape
    d = w_down_bf16.shape[1]
    return pl.pallas_call(
        _down_kernel,
        grid=(m // tm,),
        in_specs=[pl.BlockSpec((tm, hidden), lambda i: (i, 0)),
                  pl.BlockSpec((hidden, d), lambda i: (0, 0), pipeline_mode=pl.Buffered(1)),
                  pl.BlockSpec((tm, d), lambda i: (i, 0)),
                  pl.BlockSpec((1, d), lambda i: (0, 0))],
        out_specs=pl.BlockSpec((tm, d), lambda i: (i, 0)),
        out_shape=jax.ShapeDtypeStruct((m, d), jnp.float32),
        compiler_params=_params(("arbitrary",)),
        name="down",
    )(a, w_down_bf16, x1, g.reshape(1, d))


def _rotary_tables(seq):
    inv_freq = ROPE_THETA ** (-jnp.arange(0, HEAD_DIM, 2, dtype=jnp.float32) / HEAD_DIM)
    ang = jnp.arange(seq, dtype=jnp.float32)[:, None] * inv_freq[None, :]
    cos, sin = jnp.cos(ang), jnp.sin(ang)
    reps = HEAD_WIDTH // HEAD_DIM
    cos_t = jnp.tile(jnp.concatenate([cos, cos], axis=-1), (1, reps))
    sin_t = jnp.tile(jnp.concatenate([-sin, sin], axis=-1), (1, reps))
    return cos_t, sin_t


def kernel(x, norm_mix, w_in, w_pool, pool_scale, lambda_q1, lambda_k1, lambda_q2,
           lambda_k2, subln_gain, w_out, norm_ffn, w_gate_up, w_down, norm_final):
    b, s, d = x.shape
    depth = w_in.shape[0]
    assert depth == 1, "single-layer block"
    pool_width = w_pool.shape[1] * w_pool.shape[2]
    attn_width = w_out.shape[1] - pool_width
    n_heads = attn_width // HEAD_WIDTH

    cos_t, sin_t = _rotary_tables(s)
    x2d = x.reshape(b * s, d)
    l = 0
    h1, pool_out = _norm_pool(x2d, norm_mix[l], w_in[l], w_pool[l], pool_scale[l], s)
    attn_out, w_out_bf16, w_down_bf16 = _proj_attention(
        h1.reshape(b, s, d), w_in[l], cos_t, sin_t, lambda_q1[l], lambda_k1[l], lambda_q2[l],
        lambda_k2[l], subln_gain[l], w_out[l], w_down[l], q_col0=pool_width, n_heads=n_heads)
    x1, h2 = _out_proj(pool_out, attn_out.reshape(b * s, -1), x2d,
                       w_out_bf16, norm_ffn[l])
    a = _gate_up(h2, w_gate_up[l], w_down.shape[1])
    out = _down(a, w_down_bf16, x1, norm_final)
    return out.reshape(b, s, d)
```

```python
import functools
import math

import jax
import jax.numpy as jnp
from jax import lax
from jax.experimental import pallas as pl
from jax.experimental.pallas import tpu as pltpu

CHUNK = 64
POOL_WINDOWS = (2, 4, 8, 16)
HEAD_DIM = 64
HEAD_WIDTH = 2 * HEAD_DIM
ROPE_THETA = 10000.0
NORM_EPS = 1e-6
LAMBDA_INIT = 0.8 - 0.6 * math.exp(-0.3 * 0)

V7X_VMEM_LIMIT_BYTES = 56 * 1024 * 1024

F32_SUBLANES = 8
BF16_SUBLANES = 16

LOG2_E = math.log2(math.e)
PROJ_AHEAD = 2


def _params(semantics):
    return pltpu.CompilerParams(dimension_semantics=semantics,
                                vmem_limit_bytes=V7X_VMEM_LIMIT_BYTES)


def _row_subtiles(rows, n=2):
    step, rem = divmod(rows, n)
    assert rem == 0
    return [slice(r * step, (r + 1) * step) for r in range(n)]


def _rms_scale(x):
    return x * lax.rsqrt(jnp.mean(x * x, axis=-1, keepdims=True) + NORM_EPS)


POOL_HALO = 16


def _shift_rows_down(s, shift):
    if shift == F32_SUBLANES:
        return jnp.concatenate([s[:shift], s[:-shift]], axis=0)
    return pltpu.roll(s, shift, axis=0)


def _pool_group(prev, u, window, is_seq_start):
    s = jnp.concatenate([prev, u], axis=0)
    shift = 1
    while shift < window:
        s = s + _shift_rows_down(s, shift)
        shift *= 2
    s = s[POOL_HALO:]
    pooled = s * (1.0 / window) - u
    cnt = (lax.broadcasted_iota(jnp.int32, (POOL_HALO, u.shape[1]), 0) + 1).astype(jnp.float32)
    cnt = jnp.where(is_seq_start, jnp.minimum(cnt, float(window)), float(window))
    top = s[:POOL_HALO] / cnt - u[:POOL_HALO]
    return jnp.concatenate([top, pooled[POOL_HALO:]], axis=0)


def _norm_pool_kernel(x_ref, g_ref, w_ref, wp_ref, ps_ref, h_ref, o_ref,
                      wbf_ref, wpb_ref, halo_ref, *, tiles_per_seq):
    i = pl.program_id(0)

    @pl.when(i == 0)
    def _():
        wbf_ref[...] = w_ref[...].astype(jnp.bfloat16)
        wpb_ref[...] = wp_ref[...].astype(jnp.bfloat16)
        halo_ref[...] = jnp.zeros_like(halo_ref)

    parts = []
    for rows in _row_subtiles(x_ref.shape[0]):
        h = (_rms_scale(x_ref[rows, :]) * g_ref[...]).astype(jnp.bfloat16)
        h_ref[rows, :] = h
        parts.append(jnp.dot(h, wbf_ref[...], preferred_element_type=jnp.float32))
    u = jnp.concatenate(parts, axis=0)

    is_seq_start = (i % tiles_per_seq) == 0
    prev = jnp.where(is_seq_start, 0.0, halo_ref[...])
    halo_ref[...] = u[-POOL_HALO:]
    cg = wp_ref.shape[1]
    for gi, window in enumerate(POOL_WINDOWS):
        cols = slice(gi * cg, (gi + 1) * cg)
        pooled = _pool_group(prev[:, cols], u[:, cols], window, is_seq_start)
        mapped = jnp.dot(pooled.astype(jnp.bfloat16), wpb_ref[gi],
                         preferred_element_type=jnp.float32)
        o_ref[:, cols] = (mapped * ps_ref[:, cols]).astype(o_ref.dtype)


def _norm_pool(x2d, g, w, w_pool, pool_scale, seq, tm=1024):
    m, d = x2d.shape
    ng, cg, _ = w_pool.shape
    pw = ng * cg
    assert seq % tm == 0 and max(POOL_WINDOWS) <= POOL_HALO
    return pl.pallas_call(
        functools.partial(_norm_pool_kernel, tiles_per_seq=seq // tm),
        grid=(m // tm,),
        in_specs=[pl.BlockSpec((tm, d), lambda i: (i, 0)),
                  pl.BlockSpec((1, d), lambda i: (0, 0)),
                  pl.BlockSpec((d, pw), lambda i: (0, 0), pipeline_mode=pl.Buffered(1)),
                  pl.BlockSpec((ng, cg, cg), lambda i: (0, 0, 0), pipeline_mode=pl.Buffered(1)),
                  pl.BlockSpec((1, pw), lambda i: (0, 0))],
        out_specs=[pl.BlockSpec((tm, d), lambda i: (i, 0)),
                   pl.BlockSpec((tm, pw), lambda i: (i, 0))],
        out_shape=[jax.ShapeDtypeStruct((m, d), jnp.bfloat16),
                   jax.ShapeDtypeStruct((m, pw), jnp.bfloat16)],
        scratch_shapes=[pltpu.VMEM((d, pw), jnp.bfloat16),
                        pltpu.VMEM((ng, cg, cg), jnp.bfloat16),
                        pltpu.VMEM((POOL_HALO, pw), jnp.float32)],
        compiler_params=_params(("arbitrary",)),
        name="norm_pool",
    )(x2d, g.reshape(1, d), w, w_pool, pool_scale.reshape(1, pw))


def _rope(t, cos_t, sin_t):
    width = t.shape[-1]
    half = HEAD_DIM // 2
    lane = lax.broadcasted_iota(jnp.int32, t.shape, 1)
    first_half = (lane % HEAD_DIM) < half
    partner = jnp.where(first_half,
                        pltpu.roll(t, width - half, axis=1),
                        pltpu.roll(t, half, axis=1))
    return t * cos_t + partner * sin_t


def _proj_attn_kernel(h_ref, wq_ref, wk_ref, wv_ref, cos_ref, sin_ref, lq1_ref, lk1_ref,
                      lq2_ref, lk2_ref, gain_ref, wo_src_ref, wd_src_ref,
                      o_ref, wo_dst_ref, wd_dst_ref,
                      wbf_ref, q_ref, k_ref, vt_ref, *st_refs, tq, rb):
    seq = h_ref.shape[1]
    hw = HEAD_WIDTH
    gw = q_ref.shape[1]
    heads = gw // hw
    n_tiles = seq // tq
    kc = tq
    n_row_blocks = seq // rb

    wo_dst_ref[...] = wo_src_ref[...].astype(jnp.bfloat16)
    wd_dst_ref[...] = wd_src_ref[...].astype(jnp.bfloat16)

    wbf_ref[:, :gw] = wq_ref[...].astype(jnp.bfloat16)
    wbf_ref[:, gw:2 * gw] = wk_ref[...].astype(jnp.bfloat16)
    wbf_ref[:, 2 * gw:] = wv_ref[...].astype(jnp.bfloat16)

    for hd in range(heads):
        vt_ref[hd, hw:, :] = jnp.ones((vt_ref.shape[1] - hw, seq), jnp.bfloat16)

    lam = (jnp.exp(jnp.sum(lq1_ref[...] * lk1_ref[...], axis=-1, keepdims=True))
           - jnp.exp(jnp.sum(lq2_ref[...] * lk2_ref[...], axis=-1, keepdims=True))
           + LAMBDA_INIT)

    lane = lax.broadcasted_iota(jnp.int32, (tq, hw), 1)
    key = lax.broadcasted_iota(jnp.int32, (kc, 2 * tq), 0)
    qry = lax.broadcasted_iota(jnp.int32, (kc, 2 * tq), 1) % tq

    def project(r):
        rows = slice(r * rb, (r + 1) * rb)
        y = jnp.dot(h_ref[0, rows, :], wbf_ref[...], preferred_element_type=jnp.float32)
        cos_t, sin_t = cos_ref[rows, :], sin_ref[rows, :]
        q_scale = LOG2_E / math.sqrt(HEAD_DIM)
        cos_q, sin_q = cos_t * q_scale, sin_t * q_scale
        for hd in range(heads):
            cols = slice(hd * hw, (hd + 1) * hw)
            q_ref[rows, cols] = _rope(y[:, cols], cos_q, sin_q).astype(jnp.bfloat16)
            k_ref[rows, cols] = _rope(y[:, gw + hd * hw:gw + (hd + 1) * hw],
                                      cos_t, sin_t).astype(jnp.bfloat16)
            vt_ref[hd, :hw, rows] = y[:, 2 * gw + hd * hw:2 * gw + (hd + 1) * hw
                                      ].T.astype(jnp.bfloat16)

    projected = []

    def ensure_projected(r):
        while len(projected) <= min(r, n_row_blocks - 1):
            project(len(projected))
            projected.append(True)

    def masked_queries(hd, i):
        q = q_ref[i * tq:(i + 1) * tq, hd * hw:(hd + 1) * hw]
        return jnp.concatenate([jnp.where(lane < HEAD_DIM, q, 0),
                                jnp.where(lane >= HEAD_DIM, q, 0)], axis=0)

    def scores(hd, i, c, q2, slot):
        s = lax.dot_general(k_ref[c * kc:(c + 1) * kc, hd * hw:(hd + 1) * hw], q2,
                            (((1,), (1,)), ((), ())),
                            preferred_element_type=jnp.float32)
        if c == i:
            s = jnp.where((qry // CHUNK) >= (key // CHUNK), s, -jnp.inf)
        st_refs[slot][...] = s
        return jnp.max(s, axis=0, keepdims=True)

    def values(hd, c, m, slot):
        p = jnp.exp2(st_refs[slot][...] - m).astype(jnp.bfloat16)
        return jnp.dot(vt_ref[hd, :, c * kc:(c + 1) * kc], p,
                       preferred_element_type=jnp.float32)

    items = [(hd, i, c) for i in range(n_tiles) for c in range(i + 1) for hd in range(heads)]
    lookahead = len(st_refs) - 1
    q2s, refs, accs = {}, {}, {}
    m_run = {}

    def block_of(i):
        return ((i + 1) * tq - 1) // rb

    def emit_scores(n):
        hd, i, c = items[n]
        ensure_projected(block_of(i))
        if (hd, i) not in q2s:
            q2s[hd, i] = masked_queries(hd, i)
        m_c = scores(hd, i, c, q2s[hd, i], n % len(st_refs))
        if c == 0:
            refs[n] = (m_c, None)
        else:
            m_new = jnp.maximum(m_run[hd], m_c)
            refs[n] = (m_new, jnp.exp2(m_run[hd] - m_new))
        m_run[hd] = refs[n][0]

    ensure_projected(PROJ_AHEAD)
    for n in range(min(lookahead, len(items))):
        emit_scores(n)
    for n, (hd, i, c) in enumerate(items):
        ensure_projected(block_of(i) + PROJ_AHEAD)
        if n + lookahead < len(items):
            emit_scores(n + lookahead)
        m_new, alpha = refs.pop(n)
        part = values(hd, c, m_new, n % len(st_refs))
        accs[hd] = part if c == 0 else accs[hd] * alpha + part
        if c < i:
            continue
        acc = accs.pop(hd)
        ot = acc[:hw] * (1.0 / acc[hw:hw + 1])
        o = (ot[:, :tq] - lam * ot[:, tq:]).T
        o = _rms_scale(o) * gain_ref[...] * (1.0 - LAMBDA_INIT)
        o_ref[0, i * tq:(i + 1) * tq, hd * hw:(hd + 1) * hw] = o.astype(o_ref.dtype)


def _proj_attention(h3, w_in, cos_t, sin_t, lq1, lk1, lq2, lk2, gain, w_out, w_down, *,
                    q_col0, n_heads, heads_per_step=2, tq=256, rb=256, n_slots=4):
    b, s, d = h3.shape
    hw = HEAD_WIDTH
    gw = heads_per_step * hw
    width = n_heads * hw
    nj = n_heads // heads_per_step
    assert q_col0 % gw == 0 and width % gw == 0 and s % rb == 0 and rb % tq == 0
    qb, kb, vb = q_col0 // gw, (q_col0 + width) // gw, (q_col0 + 2 * width) // gw
    steps = b * nj
    wo_slab, wo_rem = divmod(w_out.shape[0], steps)
    wd_slab, wd_rem = divmod(w_down.shape[0], steps)
    assert wo_rem == 0 and wd_rem == 0
    assert wo_slab % BF16_SUBLANES == 0 and wd_slab % BF16_SUBLANES == 0
    vec = lambda a: a.reshape(1, -1)
    small = lambda n: pl.BlockSpec((1, n), lambda bi, j: (0, 0))
    table = pl.BlockSpec((s, hw), lambda bi, j: (0, 0), pipeline_mode=pl.Buffered(1))
    slab = lambda rows, cols: pl.BlockSpec((rows, cols), lambda bi, j: (bi * nj + j, 0))
    return pl.pallas_call(
        functools.partial(_proj_attn_kernel, tq=tq, rb=rb),
        grid=(b, nj),
        in_specs=[pl.BlockSpec((1, s, d), lambda bi, j: (bi, 0, 0)),
                  pl.BlockSpec((d, gw), lambda bi, j: (0, qb + j)),
                  pl.BlockSpec((d, gw), lambda bi, j: (0, kb + j)),
                  pl.BlockSpec((d, gw), lambda bi, j: (0, vb + j)),
                  table, table,
                  small(HEAD_DIM), small(HEAD_DIM), small(HEAD_DIM), small(HEAD_DIM),
                  small(hw),
                  slab(wo_slab, w_out.shape[1]), slab(wd_slab, w_down.shape[1])],
        out_specs=[pl.BlockSpec((1, s, gw), lambda bi, j: (bi, 0, j)),
                   slab(wo_slab, w_out.shape[1]), slab(wd_slab, w_down.shape[1])],
        out_shape=[jax.ShapeDtypeStruct((b, s, width), jnp.bfloat16),
                   jax.ShapeDtypeStruct(w_out.shape, jnp.bfloat16),
                   jax.ShapeDtypeStruct(w_down.shape, jnp.bfloat16)],
        scratch_shapes=[pltpu.VMEM((d, 3 * gw), jnp.bfloat16),
                        pltpu.VMEM((s, gw), jnp.bfloat16),
                        pltpu.VMEM((s, gw), jnp.bfloat16),
                        pltpu.VMEM((heads_per_step, hw + BF16_SUBLANES, s), jnp.bfloat16)]
        + [pltpu.VMEM((tq, 2 * tq), jnp.float32)] * n_slots,
        compiler_params=_params(("arbitrary", "arbitrary")),
        name="proj_attn",
    )(h3, w_in, w_in, w_in, cos_t, sin_t, vec(lq1), vec(lk1), vec(lq2), vec(lk2), vec(gain),
      w_out, w_down)


def _out_proj_kernel(p_ref, a_ref, x_ref, w_ref, g_ref, x1_ref, h_ref):
    kp = p_ref.shape[1]
    for rows in _row_subtiles(x_ref.shape[0]):
        y = jnp.dot(p_ref[rows, :], w_ref[:kp, :], preferred_element_type=jnp.float32)
        y += jnp.dot(a_ref[rows, :], w_ref[kp:, :], preferred_element_type=jnp.float32)
        x1_ref[rows, :] = x_ref[rows, :] + y
        h_ref[rows, :] = (_rms_scale(x1_ref[rows, :]) * g_ref[...]).astype(h_ref.dtype)


def _out_proj(pool_out, attn_out, x2d, w_out_bf16, g, tm=512):
    m, d = x2d.shape
    kp, ka = pool_out.shape[1], attn_out.shape[1]
    return pl.pallas_call(
        _out_proj_kernel,
        grid=(m // tm,),
        in_specs=[pl.BlockSpec((tm, kp), lambda i: (i, 0)),
                  pl.BlockSpec((tm, ka), lambda i: (i, 0)),
                  pl.BlockSpec((tm, d), lambda i: (i, 0)),
                  pl.BlockSpec((kp + ka, d), lambda i: (0, 0),
                               pipeline_mode=pl.Buffered(1)),
                  pl.BlockSpec((1, d), lambda i: (0, 0))],
        out_specs=[pl.BlockSpec((tm, d), lambda i: (i, 0)),
                   pl.BlockSpec((tm, d), lambda i: (i, 0))],
        out_shape=[jax.ShapeDtypeStruct((m, d), jnp.float32),
                   jax.ShapeDtypeStruct((m, d), jnp.bfloat16)],
        compiler_params=_params(("arbitrary",)),
        name="out_proj",
    )(pool_out, attn_out, x2d, w_out_bf16, g.reshape(1, d))


def _gate_up_kernel(a_ref, wg_ref, wu_ref, o_ref, wgb_ref, wub_ref):
    @pl.when(pl.program_id(1) == 0)
    def _():
        wgb_ref[...] = wg_ref[...].astype(jnp.bfloat16)
        wub_ref[...] = wu_ref[...].astype(jnp.bfloat16)

    for rows in _row_subtiles(a_ref.shape[0]):
        a = a_ref[rows, :]
        gate = jnp.dot(a, wgb_ref[...], preferred_element_type=jnp.float32)
        up = jnp.dot(a, wub_ref[...], preferred_element_type=jnp.float32)
        o_ref[rows, :] = (gate * jax.nn.sigmoid(gate) * up).astype(o_ref.dtype)


def _gate_up(a, w_gate_up, hidden, tm=2048, th=512):
    m, k = a.shape
    nb = hidden // th
    return pl.pallas_call(
        _gate_up_kernel,
        grid=(nb, m // tm),
        in_specs=[pl.BlockSpec((tm, k), lambda j, i: (i, 0)),
                  pl.BlockSpec((k, th), lambda j, i: (0, j)),
                  pl.BlockSpec((k, th), lambda j, i: (0, j + nb))],
        out_specs=pl.BlockSpec((tm, th), lambda j, i: (i, j)),
        out_shape=jax.ShapeDtypeStruct((m, hidden), jnp.bfloat16),
        scratch_shapes=[pltpu.VMEM((k, th), jnp.bfloat16),
                        pltpu.VMEM((k, th), jnp.bfloat16)],
        compiler_params=_params(("arbitrary", "arbitrary")),
        name="gate_up",
    )(a, w_gate_up, w_gate_up)


def _down_kernel(a_ref, w_hbm_ref, x1_ref, g_ref, o_ref, w_ref, sem_ref, *, n_chunks):
    hidden = w_ref.shape[0]
    kc = hidden // n_chunks

    def w_copy(c):
        rows = pl.ds(c * kc, kc)
        return pltpu.make_async_copy(w_hbm_ref.at[rows, :], w_ref.at[rows, :], sem_ref.at[c])

    @pl.when(pl.program_id(0) == 0)
    def _():
        for c in range(n_chunks):
            w_copy(c).start()
        for r, rows in enumerate(_row_subtiles(o_ref.shape[0])):
            acc = x1_ref[rows, :]
            for c in range(n_chunks):
                if r == 0:
                    w_copy(c).wait()
                acc += jnp.dot(a_ref[rows, c * kc:(c + 1) * kc], w_ref[c * kc:(c + 1) * kc, :],
                               preferred_element_type=jnp.float32)
            o_ref[rows, :] = _rms_scale(acc) * g_ref[...]

    @pl.when(pl.program_id(0) > 0)
    def _():
        for rows in _row_subtiles(o_ref.shape[0]):
            o_ref[rows, :] = x1_ref[rows, :] + jnp.dot(a_ref[rows, :], w_ref[...],
                                                       preferred_element_type=jnp.float32)
            o_ref[rows, :] = _rms_scale(o_ref[rows, :]) * g_ref[...]


def _down(a, w_down_bf16, x1, g, tm=512, n_chunks=4):
    m, hidden = a.shape
    d = w_down_bf16.shape[1]
    assert hidden % n_chunks == 0 and (hidden // n_chunks) % 128 == 0
    return pl.pallas_call(
        functools.partial(_down_kernel, n_chunks=n_chunks),
        grid=(m // tm,),
        in_specs=[pl.BlockSpec((tm, hidden), lambda i: (i, 0)),
                  pl.BlockSpec(memory_space=pl.ANY),
                  pl.BlockSpec((tm, d), lambda i: (i, 0)),
                  pl.BlockSpec((1, d), lambda i: (0, 0))],
        out_specs=pl.BlockSpec((tm, d), lambda i: (i, 0)),
        out_shape=jax.ShapeDtypeStruct((m, d), jnp.float32),
        scratch_shapes=[pltpu.VMEM((hidden, d), jnp.bfloat16),
                        pltpu.SemaphoreType.DMA((n_chunks,))],
        compiler_params=_params(("arbitrary",)),
        name="down",
    )(a, w_down_bf16, x1, g.reshape(1, d))


def _rotary_tables(seq):
    inv_freq = ROPE_THETA ** (-jnp.arange(0, HEAD_DIM, 2, dtype=jnp.float32) / HEAD_DIM)
    ang = jnp.arange(seq, dtype=jnp.float32)[:, None] * inv_freq[None, :]
    cos, sin = jnp.cos(ang), jnp.sin(ang)
    reps = HEAD_WIDTH // HEAD_DIM
    cos_t = jnp.tile(jnp.concatenate([cos, cos], axis=-1), (1, reps))
    sin_t = jnp.tile(jnp.concatenate([-sin, sin], axis=-1), (1, reps))
    return cos_t, sin_t


def kernel(x, norm_mix, w_in, w_pool, pool_scale, lambda_q1, lambda_k1, lambda_q2,
           lambda_k2, subln_gain, w_out, norm_ffn, w_gate_up, w_down, norm_final):
    b, s, d = x.shape
    depth = w_in.shape[0]
    assert depth == 1, "single-layer block"
    pool_width = w_pool.shape[1] * w_pool.shape[2]
    attn_width = w_out.shape[1] - pool_width
    n_heads = attn_width // HEAD_WIDTH

    cos_t, sin_t = _rotary_tables(s)
    x2d = x.reshape(b * s, d)
    l = 0
    h1, pool_out = _norm_pool(x2d, norm_mix[l], w_in[l], w_pool[l], pool_scale[l], s)
    attn_out, w_out_bf16, w_down_bf16 = _proj_attention(
        h1.reshape(b, s, d), w_in[l], cos_t, sin_t, lambda_q1[l], lambda_k1[l], lambda_q2[l],
        lambda_k2[l], subln_gain[l], w_out[l], w_down[l], q_col0=pool_width, n_heads=n_heads)
    x1, h2 = _out_proj(pool_out, attn_out.reshape(b * s, -1), x2d,
                       w_out_bf16, norm_ffn[l])
    a = _gate_up(h2, w_gate_up[l], w_down.shape[1])
    out = _down(a, w_down_bf16, x1, norm_final)
    return out.reshape(b, s, d)
```

```python
import functools
import math

import jax
import jax.numpy as jnp
from jax import lax
from jax.experimental import pallas as pl
from jax.experimental.pallas import tpu as pltpu

CHUNK = 64
POOL_WINDOWS = (2, 4, 8, 16)
HEAD_DIM = 64
HEAD_WIDTH = 2 * HEAD_DIM
ROPE_THETA = 10000.0
NORM_EPS = 1e-6
LAMBDA_INIT = 0.8 - 0.6 * math.exp(-0.3 * 0)

V7X_VMEM_LIMIT_BYTES = 56 * 1024 * 1024

F32_SUBLANES = 8
BF16_SUBLANES = 16

LOG2_E = math.log2(math.e)
PROJ_AHEAD = 2


def _params(semantics):
    return pltpu.CompilerParams(dimension_semantics=semantics,
                                vmem_limit_bytes=V7X_VMEM_LIMIT_BYTES)


def _row_subtiles(rows, n=2):
    step, rem = divmod(rows, n)
    assert rem == 0
    return [slice(r * step, (r + 1) * step) for r in range(n)]


def _rms_scale(x):
    return x * lax.rsqrt(jnp.mean(x * x, axis=-1, keepdims=True) + NORM_EPS)


POOL_HALO = 16


def _shift_rows_down(s, shift):
    if shift == F32_SUBLANES:
        return jnp.concatenate([s[:shift], s[:-shift]], axis=0)
    return pltpu.roll(s, shift, axis=0)


def _pool_group(prev, u, window, is_seq_start):
    s = jnp.concatenate([prev, u], axis=0)
    shift = 1
    while shift < window:
        s = s + _shift_rows_down(s, shift)
        shift *= 2
    s = s[POOL_HALO:]
    pooled = s * (1.0 / window) - u
    cnt = (lax.broadcasted_iota(jnp.int32, (POOL_HALO, u.shape[1]), 0) + 1).astype(jnp.float32)
    cnt = jnp.where(is_seq_start, jnp.minimum(cnt, float(window)), float(window))
    top = s[:POOL_HALO] / cnt - u[:POOL_HALO]
    return jnp.concatenate([top, pooled[POOL_HALO:]], axis=0)


def _norm_pool_kernel(x_ref, g_ref, w_ref, wp_ref, ps_ref, h_ref, o_ref,
                      wbf_ref, wpb_ref, halo_ref, *, tiles_per_seq):
    i = pl.program_id(0)

    @pl.when(i == 0)
    def _():
        wbf_ref[...] = w_ref[...].astype(jnp.bfloat16)
        wpb_ref[...] = wp_ref[...].astype(jnp.bfloat16)
        halo_ref[...] = jnp.zeros_like(halo_ref)

    parts = []
    for rows in _row_subtiles(x_ref.shape[0]):
        h = (_rms_scale(x_ref[rows, :]) * g_ref[...]).astype(jnp.bfloat16)
        h_ref[rows, :] = h
        parts.append(jnp.dot(h, wbf_ref[...], preferred_element_type=jnp.float32))
    u = jnp.concatenate(parts, axis=0)

    is_seq_start = (i % tiles_per_seq) == 0
    prev = jnp.where(is_seq_start, 0.0, halo_ref[...])
    halo_ref[...] = u[-POOL_HALO:]
    cg = wp_ref.shape[1]
    for gi, window in enumerate(POOL_WINDOWS):
        cols = slice(gi * cg, (gi + 1) * cg)
        pooled = _pool_group(prev[:, cols], u[:, cols], window, is_seq_start)
        mapped = jnp.dot(pooled.astype(jnp.bfloat16), wpb_ref[gi],
                         preferred_element_type=jnp.float32)
        o_ref[:, cols] = (mapped * ps_ref[:, cols]).astype(o_ref.dtype)


def _norm_pool(x2d, g, w, w_pool, pool_scale, seq, tm=1024):
    m, d = x2d.shape
    ng, cg, _ = w_pool.shape
    pw = ng * cg
    assert seq % tm == 0 and max(POOL_WINDOWS) <= POOL_HALO
    return pl.pallas_call(
        functools.partial(_norm_pool_kernel, tiles_per_seq=seq // tm),
        grid=(m // tm,),
        in_specs=[pl.BlockSpec((tm, d), lambda i: (i, 0)),
                  pl.BlockSpec((1, d), lambda i: (0, 0)),
                  pl.BlockSpec((d, pw), lambda i: (0, 0), pipeline_mode=pl.Buffered(1)),
                  pl.BlockSpec((ng, cg, cg), lambda i: (0, 0, 0), pipeline_mode=pl.Buffered(1)),
                  pl.BlockSpec((1, pw), lambda i: (0, 0))],
        out_specs=[pl.BlockSpec((tm, d), lambda i: (i, 0)),
                   pl.BlockSpec((tm, pw), lambda i: (i, 0))],
        out_shape=[jax.ShapeDtypeStruct((m, d), jnp.bfloat16),
                   jax.ShapeDtypeStruct((m, pw), jnp.bfloat16)],
        scratch_shapes=[pltpu.VMEM((d, pw), jnp.bfloat16),
                        pltpu.VMEM((ng, cg, cg), jnp.bfloat16),
                        pltpu.VMEM((POOL_HALO, pw), jnp.float32)],
        compiler_params=_params(("arbitrary",)),
        name="norm_pool",
    )(x2d, g.reshape(1, d), w, w_pool, pool_scale.reshape(1, pw))


def _rope(t, cos_t, sin_t):
    width = t.shape[-1]
    half = HEAD_DIM // 2
    lane = lax.broadcasted_iota(jnp.int32, t.shape, 1)
    first_half = (lane % HEAD_DIM) < half
    partner = jnp.where(first_half,
                        pltpu.roll(t, width - half, axis=1),
                        pltpu.roll(t, half, axis=1))
    return t * cos_t + partner * sin_t


def _proj_attn_kernel(h_ref, wq_ref, wk_ref, wv_ref, cos_ref, sin_ref, lq1_ref, lk1_ref,
                      lq2_ref, lk2_ref, gain_ref, wo_src_ref, wd_src_ref,
                      o_ref, wo_dst_ref, wd_dst_ref,
                      wbf_ref, q_ref, k_ref, vt_ref, *st_refs, tq, rb):
    seq = h_ref.shape[1]
    hw = HEAD_WIDTH
    gw = q_ref.shape[1]
    heads = gw // hw
    n_tiles = seq // tq
    kc = tq
    n_row_blocks = seq // rb

    wo_dst_ref[...] = wo_src_ref[...].astype(jnp.bfloat16)
    wd_dst_ref[...] = wd_src_ref[...].astype(jnp.bfloat16)

    wbf_ref[:, :gw] = wq_ref[...].astype(jnp.bfloat16)
    wbf_ref[:, gw:2 * gw] = wk_ref[...].astype(jnp.bfloat16)
    wbf_ref[:, 2 * gw:] = wv_ref[...].astype(jnp.bfloat16)

    for hd in range(heads):
        vt_ref[hd, hw:, :] = jnp.ones((vt_ref.shape[1] - hw, seq), jnp.bfloat16)

    lam = (jnp.exp(jnp.sum(lq1_ref[...] * lk1_ref[...], axis=-1, keepdims=True))
           - jnp.exp(jnp.sum(lq2_ref[...] * lk2_ref[...], axis=-1, keepdims=True))
           + LAMBDA_INIT)

    lane = lax.broadcasted_iota(jnp.int32, (tq, hw), 1)
    key = lax.broadcasted_iota(jnp.int32, (kc, 2 * tq), 0)
    qry = lax.broadcasted_iota(jnp.int32, (kc, 2 * tq), 1) % tq

    def project(r):
        rows = slice(r * rb, (r + 1) * rb)
        y = jnp.dot(h_ref[0, rows, :], wbf_ref[...], preferred_element_type=jnp.float32)
        cos_t, sin_t = cos_ref[rows, :], sin_ref[rows, :]
        q_scale = LOG2_E / math.sqrt(HEAD_DIM)
        cos_q, sin_q = cos_t * q_scale, sin_t * q_scale
        for hd in range(heads):
            cols = slice(hd * hw, (hd + 1) * hw)
            q_ref[rows, cols] = _rope(y[:, cols], cos_q, sin_q).astype(jnp.bfloat16)
            k_ref[rows, cols] = _rope(y[:, gw + hd * hw:gw + (hd + 1) * hw],
                                      cos_t, sin_t).astype(jnp.bfloat16)
            vt_ref[hd, :hw, rows] = y[:, 2 * gw + hd * hw:2 * gw + (hd + 1) * hw
                                      ].T.astype(jnp.bfloat16)

    projected = []

    def ensure_projected(r):
        while len(projected) <= min(r, n_row_blocks - 1):
            project(len(projected))
            projected.append(True)

    def masked_queries(hd, i):
        q = q_ref[i * tq:(i + 1) * tq, hd * hw:(hd + 1) * hw]
        return jnp.concatenate([jnp.where(lane < HEAD_DIM, q, 0),
                                jnp.where(lane >= HEAD_DIM, q, 0)], axis=0)

    def scores(hd, i, c, q2, slot):
        s = lax.dot_general(k_ref[c * kc:(c + 1) * kc, hd * hw:(hd + 1) * hw], q2,
                            (((1,), (1,)), ((), ())),
                            preferred_element_type=jnp.float32)
        if c == i:
            s = jnp.where((qry // CHUNK) >= (key // CHUNK), s, -jnp.inf)
        st_refs[slot][...] = s
        return jnp.max(s, axis=0, keepdims=True)

    def values(hd, c, m, slot):
        p = jnp.exp2(st_refs[slot][...] - m).astype(jnp.bfloat16)
        return jnp.dot(vt_ref[hd, :, c * kc:(c + 1) * kc], p,
                       preferred_element_type=jnp.float32)

    items = [(hd, i, c) for i in range(n_tiles) for c in range(i + 1) for hd in range(heads)]
    lookahead = len(st_refs) - 1
    q2s, refs, accs = {}, {}, {}
    m_run = {}

    def block_of(i):
        return ((i + 1) * tq - 1) // rb

    def emit_scores(n):
        hd, i, c = items[n]
        ensure_projected(block_of(i))
        if (hd, i) not in q2s:
            q2s[hd, i] = masked_queries(hd, i)
        m_c = scores(hd, i, c, q2s[hd, i], n % len(st_refs))
        if c == 0:
            refs[n] = (m_c, None)
        else:
            m_new = jnp.maximum(m_run[hd], m_c)
            refs[n] = (m_new, jnp.exp2(m_run[hd] - m_new))
        m_run[hd] = refs[n][0]

    ensure_projected(PROJ_AHEAD)
    for n in range(min(lookahead, len(items))):
        emit_scores(n)
    for n, (hd, i, c) in enumerate(items):
        ensure_projected(block_of(i) + PROJ_AHEAD)
        if n + lookahead < len(items):
            emit_scores(n + lookahead)
        m_new, alpha = refs.pop(n)
        part = values(hd, c, m_new, n % len(st_refs))
        accs[hd] = part if c == 0 else accs[hd] * alpha + part
        if c < i:
            continue
        acc = accs.pop(hd)
        ot = acc[:hw] * (1.0 / acc[hw:hw + 1])
        o = (ot[:, :tq] - lam * ot[:, tq:]).T
        o = _rms_scale(o) * gain_ref[...] * (1.0 - LAMBDA_INIT)
        o_ref[0, i * tq:(i + 1) * tq, hd * hw:(hd + 1) * hw] = o.astype(o_ref.dtype)


def _proj_attention(h3, w_in, cos_t, sin_t, lq1, lk1, lq2, lk2, gain, w_out, w_down, *,
                    q_col0, n_heads, heads_per_step=2, tq=256, rb=256, n_slots=4):
    b, s, d = h3.shape
    hw = HEAD_WIDTH
    gw = heads_per_step * hw
    width = n_heads * hw
    nj = n_heads // heads_per_step
    assert q_col0 % gw == 0 and width % gw == 0 and s % rb == 0 and rb % tq == 0
    qb, kb, vb = q_col0 // gw, (q_col0 + width) // gw, (q_col0 + 2 * width) // gw
    steps = b * nj
    wo_slab, wo_rem = divmod(w_out.shape[0], steps)
    wd_slab, wd_rem = divmod(w_down.shape[0], steps)
    assert wo_rem == 0 and wd_rem == 0
    assert wo_slab % BF16_SUBLANES == 0 and wd_slab % BF16_SUBLANES == 0
    vec = lambda a: a.reshape(1, -1)
    small = lambda n: pl.BlockSpec((1, n), lambda bi, j: (0, 0))
    table = pl.BlockSpec((s, hw), lambda bi, j: (0, 0), pipeline_mode=pl.Buffered(1))
    slab = lambda rows, cols: pl.BlockSpec((rows, cols), lambda bi, j: (bi * nj + j, 0))
    return pl.pallas_call(
        functools.partial(_proj_attn_kernel, tq=tq, rb=rb),
        grid=(b, nj),
        in_specs=[pl.BlockSpec((1, s, d), lambda bi, j: (bi, 0, 0)),
                  pl.BlockSpec((d, gw), lambda bi, j: (0, qb + j)),
                  pl.BlockSpec((d, gw), lambda bi, j: (0, kb + j)),
                  pl.BlockSpec((d, gw), lambda bi, j: (0, vb + j)),
                  table, table,
                  small(HEAD_DIM), small(HEAD_DIM), small(HEAD_DIM), small(HEAD_DIM),
                  small(hw),
                  slab(wo_slab, w_out.shape[1]), slab(wd_slab, w_down.shape[1])],
        out_specs=[pl.BlockSpec((1, s, gw), lambda bi, j: (bi, 0, j)),
                   slab(wo_slab, w_out.shape[1]), slab(wd_slab, w_down.shape[1])],
        out_shape=[jax.ShapeDtypeStruct((b, s, width), jnp.bfloat16),
                   jax.ShapeDtypeStruct(w_out.shape, jnp.bfloat16),
                   jax.ShapeDtypeStruct(w_down.shape, jnp.bfloat16)],
        scratch_shapes=[pltpu.VMEM((d, 3 * gw), jnp.bfloat16),
                        pltpu.VMEM((s, gw), jnp.bfloat16),
                        pltpu.VMEM((s, gw), jnp.bfloat16),
                        pltpu.VMEM((heads_per_step, hw + BF16_SUBLANES, s), jnp.bfloat16)]
        + [pltpu.VMEM((tq, 2 * tq), jnp.float32)] * n_slots,
        compiler_params=_params(("arbitrary", "arbitrary")),
        name="proj_attn",
    )(h3, w_in, w_in, w_in, cos_t, sin_t, vec(lq1), vec(lk1), vec(lq2), vec(lk2), vec(gain),
      w_out, w_down)


def _out_proj_kernel(p_ref, a_ref, x_ref, w_ref, g_ref, x1_ref, h_ref):
    kp = p_ref.shape[1]
    for rows in _row_subtiles(x_ref.shape[0]):
        y = jnp.dot(p_ref[rows, :], w_ref[:kp, :], preferred_element_type=jnp.float32)
        y += jnp.dot(a_ref[rows, :], w_ref[kp:, :], preferred_element_type=jnp.float32)
        x1_ref[rows, :] = x_ref[rows, :] + y
        h_ref[rows, :] = (_rms_scale(x1_ref[rows, :]) * g_ref[...]).astype(h_ref.dtype)


def _out_proj(pool_out, attn_out, x2d, w_out_bf16, g, tm=512):
    m, d = x2d.shape
    kp, ka = pool_out.shape[1], attn_out.shape[1]
    return pl.pallas_call(
        _out_proj_kernel,
        grid=(m // tm,),
        in_specs=[pl.BlockSpec((tm, kp), lambda i: (i, 0)),
                  pl.BlockSpec((tm, ka), lambda i: (i, 0)),
                  pl.BlockSpec((tm, d), lambda i: (i, 0)),
                  pl.BlockSpec((kp + ka, d), lambda i: (0, 0),
                               pipeline_mode=pl.Buffered(1)),
                  pl.BlockSpec((1, d), lambda i: (0, 0))],
        out_specs=[pl.BlockSpec((tm, d), lambda i: (i, 0)),
                   pl.BlockSpec((tm, d), lambda i: (i, 0))],
        out_shape=[jax.ShapeDtypeStruct((m, d), jnp.float32),
                   jax.ShapeDtypeStruct((m, d), jnp.bfloat16)],
        compiler_params=_params(("arbitrary",)),
        name="out_proj",
    )(pool_out, attn_out, x2d, w_out_bf16, g.reshape(1, d))


def _gate_up_kernel(a_ref, wg_ref, wu_ref, o_ref, wgb_ref, wub_ref):
    @pl.when(pl.program_id(1) == 0)
    def _():
        wgb_ref[...] = wg_ref[...].astype(jnp.bfloat16)
        wub_ref[...] = wu_ref[...].astype(jnp.bfloat16)

    for rows in _row_subtiles(a_ref.shape[0]):
        a = a_ref[rows, :]
        gate = jnp.dot(a, wgb_ref[...], preferred_element_type=jnp.float32)
        up = jnp.dot(a, wub_ref[...], preferred_element_type=jnp.float32)
        o_ref[rows, :] = (gate * jax.nn.sigmoid(gate) * up).astype(o_ref.dtype)


def _gate_up(a, w_gate_up, hidden, tm=2048, th=512):
    m, k = a.shape
    nb = hidden // th
    return pl.pallas_call(
        _gate_up_kernel,
        grid=(nb, m // tm),
        in_specs=[pl.BlockSpec((tm, k), lambda j, i: (i, 0)),
                  pl.BlockSpec((k, th), lambda j, i: (0, j)),
                  pl.BlockSpec((k, th), lambda j, i: (0, j + nb))],
        out_specs=pl.BlockSpec((tm, th), lambda j, i: (i, j)),
        out_shape=jax.ShapeDtypeStruct((m, hidden), jnp.bfloat16),
        scratch_shapes=[pltpu.VMEM((k, th), jnp.bfloat16),
                        pltpu.VMEM((k, th), jnp.bfloat16)],
        compiler_params=_params(("arbitrary", "arbitrary")),
        name="gate_up",
    )(a, w_gate_up, w_gate_up)


def _down_kernel(a_ref, w_ref, x1_ref, g_ref, o_ref):
    for rows in _row_subtiles(o_ref.shape[0]):
        o_ref[rows, :] = x1_ref[rows, :] + jnp.dot(a_ref[rows, :], w_ref[...],
                                                   preferred_element_type=jnp.float32)
        o_ref[rows, :] = _rms_scale(o_ref[rows, :]) * g_ref[...]


def _down(a, w_down_bf16, x1, g, tm=512):
    m, hidden = a.shape
    d = w_down_bf16.shape[1]
    return pl.pallas_call(
        _down_kernel,
        grid=(m // tm,),
        in_specs=[pl.BlockSpec((tm, hidden), lambda i: (i, 0)),
                  pl.BlockSpec((hidden, d), lambda i: (0, 0), pipeline_mode=pl.Buffered(1)),
                  pl.BlockSpec((tm, d), lambda i: (i, 0)),
                  pl.BlockSpec((1, d), lambda i: (0, 0))],
        out_specs=pl.BlockSpec((tm, d), lambda i: (i, 0)),
        out_shape=jax.ShapeDtypeStruct((m, d), jnp.float32),
        compiler_params=_params(("arbitrary",)),
        name="down",
    )(a, w_down_bf16, x1, g.reshape(1, d))


def _rotary_tables(seq):
    half = HEAD_DIM // 2
    inv_freq = ROPE_THETA ** (-jnp.arange(0, HEAD_DIM, 2, dtype=jnp.float32) / HEAD_DIM)
    lane = jnp.arange(HEAD_WIDTH)
    inv_lane = inv_freq[lane % half]
    sign = jnp.where((lane % HEAD_DIM) < half, -1.0, 1.0).astype(jnp.float32)
    ang = jnp.arange(seq, dtype=jnp.float32)[:, None] * inv_lane[None, :]
    return jnp.cos(ang), jnp.sin(ang) * sign[None, :]


def kernel(x, norm_mix, w_in, w_pool, pool_scale, lambda_q1, lambda_k1, lambda_q2,
           lambda_k2, subln_gain, w_out, norm_ffn, w_gate_up, w_down, norm_final):
    b, s, d = x.shape
    depth = w_in.shape[0]
    assert depth == 1, "single-layer block"
    pool_width = w_pool.shape[1] * w_pool.shape[2]
    attn_width = w_out.shape[1] - pool_width
    n_heads = attn_width // HEAD_WIDTH

    cos_t, sin_t = _rotary_tables(s)
    x2d = x.reshape(b * s, d)
    l = 0
    h1, pool_out = _norm_pool(x2d, norm_mix[l], w_in[l], w_pool[l], pool_scale[l], s)
    attn_out, w_out_bf16, w_down_bf16 = _proj_attention(
        h1.reshape(b, s, d), w_in[l], cos_t, sin_t, lambda_q1[l], lambda_k1[l], lambda_q2[l],
        lambda_k2[l], subln_gain[l], w_out[l], w_down[l], q_col0=pool_width, n_heads=n_heads)
    x1, h2 = _out_proj(pool_out, attn_out.reshape(b * s, -1), x2d,
                       w_out_bf16, norm_ffn[l])
    a = _gate_up(h2, w_gate_up[l], w_down.shape[1])
    out = _down(a, w_down_bf16, x1, norm_final)
    return out.reshape(b, s, d)
```
